```python
import math
import jax, jax.numpy as jnp
from jax import lax
import numpy as np


D_MODEL = 4096
BATCH = 4
SEQ = 2048
DEPTH = 4
DEC_BATCH = 8
DEC_SEQ = 4
PAST_LEN = 8192
PAGE_SIZE = 128

MIX_W = D_MODEL
A_W = MIX_W // 4
A_HD = 128
A_HEADS = A_W // A_HD
A_CHUNK = 64
B_W = MIX_W // 4
B_HD = 128
B_HEADS = B_W // B_HD
B_CONV = 4
LRU_C = 8.0
C_W = MIX_W // 4
C_HD = 128
C_HEADS = C_W // C_HD
ROPE_DIM = C_HD // 4
ROPE_THETA = 500000.0
C_PATTERNS = ((128, 1), (512, 4), (2048, 16))
C_WIN = 2048
C_QBLOCK = 128
D_W = MIX_W - A_W - B_W - C_W
D_HD = 64
D_HEADS = D_W // D_HD
D_RANK_W = 64
D_RANK_A = 64
D_RANK_G = 160
D_SHIFT_W = 3 * D_W + D_RANK_W + D_RANK_A + D_RANK_G
IN_WIDTHS = (A_W, A_W, A_W, A_W, B_W, B_W, C_W, C_W, C_W, D_SHIFT_W)
N_IN = sum(IN_WIDTHS)
D_FF = 11008
FFN_CONV = 3
RMS_EPS = 1e-6
RWKV_GN_EPS = 64e-5
NEG_BIG = -30000.0
F_FLOOR = 1e-20

kernel_name = "hybrid_hgrn2_rglru_dilatedattn_rwkv7_step"


def _mm(a, w):
    return jnp.matmul(a.astype(w.dtype), w, preferred_element_type=jnp.float32)


def _rmsnorm(x, g, eps=RMS_EPS):
    xf = x.astype(jnp.float32)
    y = xf * lax.rsqrt(jnp.mean(xf * xf, axis=-1, keepdims=True) + eps)
    return y * g.astype(jnp.float32)


def _causal_dwconv(x, buf, w):
    width = w.shape[0]
    T = x.shape[1]
    xp = jnp.concatenate([buf.astype(x.dtype), x], axis=1)
    y = xp[:, 0:T] * w[0]
    for j in range(1, width):
        y = y + xp[:, j:j + T] * w[j]
    return y, xp[:, xp.shape[1] - (width - 1):]


def _partial_rope(x, pos):
    half = ROPE_DIM // 2
    inv = ROPE_THETA ** (-jnp.arange(half, dtype=jnp.float32) / half)
    ang = pos.astype(jnp.float32)[:, None] * inv[None, :]
    cos = jnp.cos(ang)[None, :, None, :]
    sin = jnp.sin(ang)[None, :, None, :]
    x1, x2, rest = x[..., :half], x[..., half:ROPE_DIM], x[..., ROPE_DIM:]
    return jnp.concatenate([x1 * cos - x2 * sin, x2 * cos + x1 * sin, rest], axis=-1)


def _dilated_attention(q, k, v, q_loc):
    B, Q, H, Dh = q.shape
    qb = min(Q, C_QBLOCK)
    nb = Q // qb
    qs = (q * (Dh ** -0.5)).reshape(B, nb, qb, H, Dh).transpose(1, 0, 2, 3, 4)
    locs = q_loc.reshape(nb, qb)

    def block(args):
        qblk, loc = args
        lses, outs = [], []
        for win, dil in C_PATTERNS:
            n_k = win // dil + 1
            idx = loc[:, None] - dil * jnp.arange(n_k, dtype=jnp.int32)[None, :]
            valid = idx >= 0
            idx = jnp.maximum(idx, 0)
            kg = jnp.take(k, idx, axis=1)
            vg = jnp.take(v, idx, axis=1)
            s = jnp.einsum('bqhd,bqnhd->bhqn', qblk, kg)
            s = jnp.where(valid[None, None], s, NEG_BIG)
            m = jnp.max(s, axis=-1, keepdims=True)
            p = jnp.exp(s - m)
            l = jnp.sum(p, axis=-1)
            o = jnp.einsum('bhqn,bqnhd->bqhd', p, vg) / l.transpose(0, 2, 1)[..., None]
            lses.append(m[..., 0] + jnp.log(l))
            outs.append(o)
        wts = jax.nn.softmax(jnp.stack(lses), axis=0)
        wts = wts.transpose(0, 1, 3, 2)[..., None]
        return jnp.sum(wts * jnp.stack(outs), axis=0)

    out = lax.map(block, (qs, locs))
    return out.transpose(1, 0, 2, 3, 4).reshape(B, Q, H, Dh)


def _hgrn2(q, f_logit, i, g, S0, lb, norm_g):
    B, T, _ = q.shape
    sh = (B, T, A_HEADS, A_HD)
    q = jax.nn.silu(q).reshape(sh) * (A_HD ** -0.5)
    f_logit = f_logit.reshape(sh)
    lb = lb.astype(jnp.float32).reshape(A_HEADS, A_HD)
    sig = jax.nn.sigmoid(f_logit)
    f = lb + (1.0 - lb) * sig
    log_f = jnp.log(jnp.maximum(f, F_FLOOR))
    k = (1.0 - lb) * jax.nn.sigmoid(-f_logit)
    v = i.reshape(sh)
    c = math.gcd(T, A_CHUNK)
    n = T // c

    def to_chunks(t):
        return t.reshape(B, n, c, A_HEADS, A_HD).transpose(1, 0, 2, 3, 4)

    tri = jnp.tril(jnp.ones((c, c), dtype=bool))

    def step(S, inp):
        qc, kc, vc, lc = inp
        G = jnp.cumsum(lc, axis=1)
        diff = G[:, :, None] - G[:, None, :]
        decay = jnp.exp(jnp.where(tri[None, :, :, None, None], diff, NEG_BIG))
        A = jnp.einsum('bthk,bshk,btshk->bhts', qc, kc, decay)
        o = (jnp.einsum('bhts,bshv->bthv', A, vc)
             + jnp.einsum('bthk,bhkv->bthv', qc * jnp.exp(G), S))
        G_last = G[:, -1]
        S = (jnp.exp(G_last)[..., None] * S
             + jnp.einsum('bshk,bshv->bhkv', kc * jnp.exp(G_last[:, None] - G), vc))
        return S, o

    S_fin, o = lax.scan(step, S0, (to_chunks(q), to_chunks(k), to_chunks(v), to_chunks(log_f)))
    o = o.transpose(1, 0, 2, 3, 4).reshape(sh)
    o = _rmsnorm(o, norm_g) * jax.nn.silu(g.reshape(sh))
    return o.reshape(B, T, A_W), S_fin


def _rglru(xb, gb, conv_buf, h0, p):
    B, T, _ = xb.shape
    f32 = jnp.float32
    xc, new_buf = _causal_dwconv(xb, conv_buf, p['lru_conv_w'].astype(f32))
    xc = xc + p['lru_conv_b'].astype(f32)
    xh = xc.reshape(B, T, B_HEADS, B_HD)
    r = jax.nn.sigmoid(jnp.einsum('bthi,hij->bthj', xh, p['lru_wa'].astype(f32)).reshape(B, T, B_W)
                       + p['lru_ba'].astype(f32))
    ig = jax.nn.sigmoid(jnp.einsum('bthi,hij->bthj', xh, p['lru_wx'].astype(f32)).reshape(B, T, B_W)
                        + p['lru_bx'].astype(f32))
    log_a = -LRU_C * r * jax.nn.softplus(-p['lru_lambda'].astype(f32))
    a = jnp.exp(log_a)
    b = jnp.sqrt(jnp.maximum(-jnp.expm1(2.0 * log_a), 0.0)) * (ig * xc)
    b = b.at[:, 0].add(a[:, 0] * h0)

    def combine(left, right):
        a_l, b_l = left
        a_r, b_r = right
        return a_l * a_r, a_r * b_l + b_r

    _, h = lax.associative_scan(combine, (a, b), axis=1)
    y = h * jax.nn.gelu(gb, approximate=True)
    return y, h[:, -1], new_buf


def _rwkv7(P, shift_prev, S0, p):
    B, T, _ = P.shape
    f32 = jnp.float32
    P_prev = jnp.concatenate([shift_prev[:, None].astype(P.dtype), P[:, :-1]], axis=1)
    Px = P + p['rwkv_mu'].astype(f32) * (P_prev - P)
    offs = [D_W, 2 * D_W, 3 * D_W, 3 * D_W + D_RANK_W, 3 * D_W + D_RANK_W + D_RANK_A]
    r, k, v, wl, al, gl = jnp.split(Px, offs, axis=-1)
    log_w = -jax.nn.softplus(-(p['rwkv_w0'].astype(f32) + _mm(jnp.tanh(wl), p['rwkv_w2']))) - 0.5
    decay = jnp.exp(-jnp.exp(log_w))
    a = jax.nn.sigmoid(p['rwkv_a0'].astype(f32) + _mm(al, p['rwkv_a2']))
    g = _mm(jax.nn.sigmoid(gl), p['rwkv_g2'])
    hs = (B, T, D_HEADS, D_HD)
    kk = (k * p['rwkv_k_k'].astype(f32)).reshape(hs)
    kk = kk / jnp.maximum(jnp.sqrt(jnp.sum(kk * kk, axis=-1, keepdims=True)), 1e-12)
    k = k * (1.0 + (a - 1.0) * p['rwkv_k_a'].astype(f32))
    r_h, k_h, v_h, w_h = r.reshape(hs), k.reshape(hs), v.reshape(hs), decay.reshape(hs)
    z = -kk
    bvec = kk * a.reshape(hs)

    def seq(t):
        return jnp.moveaxis(t, 1, 0)

    def step(S, inp):
        r_t, w_t, k_t, v_t, z_t, b_t = inp
        sz = jnp.einsum('bhvk,bhk->bhv', S, z_t)
        S = (S * w_t[:, :, None, :] + sz[..., None] * b_t[:, :, None, :]
             + v_t[..., None] * k_t[:, :, None, :])
        return S, jnp.einsum('bhvk,bhk->bhv', S, r_t)

    S_fin, o = lax.scan(step, S0, (seq(r_h), seq(w_h), seq(k_h), seq(v_h), seq(z), seq(bvec)))
    o = jnp.moveaxis(o, 0, 1)
    mu = jnp.mean(o, axis=-1, keepdims=True)
    var = jnp.mean(jnp.square(o - mu), axis=-1, keepdims=True)
    on = ((o - mu) * lax.rsqrt(var + RWKV_GN_EPS)).reshape(B, T, D_W)
    on = on * p['rwkv_ln_g'].astype(f32) + p['rwkv_ln_b'].astype(f32)
    bonus = jnp.sum(r_h * k_h * p['rwkv_r_k'].astype(f32).reshape(D_HEADS, D_HD), axis=-1,
                    keepdims=True) * v_h
    out = (on + bonus.reshape(B, T, D_W)) * g
    return out, S_fin, P[:, -1]


def _layer(x, pos, state, lb, p):
    k_past, v_past, S_a, h_b, buf_b, S_d, shift_d, buf_f = state
    f32 = jnp.float32
    B, T, _ = x.shape
    h = _rmsnorm(x, p['ln_mix_g'])
    proj = _mm(h, p['w_in'])
    splits = [int(s) for s in np.cumsum(IN_WIDTHS)[:-1]]
    a_q, a_f, a_i, a_g, b_x, b_g, c_q, c_k, c_v, d_p = jnp.split(proj, splits, axis=-1)
    y_a, S_a = _hgrn2(a_q, a_f, a_i, a_g, S_a.astype(f32), lb, p['hgrn_norm_g'])
    y_b, h_b, buf_b = _rglru(b_x, b_g, buf_b.astype(f32), h_b.astype(f32), p)
    hs = (B, T, C_HEADS, C_HD)
    q = _partial_rope(_rmsnorm(c_q.reshape(hs), p['attn_q_norm_g']), pos)
    k = _partial_rope(_rmsnorm(c_k.reshape(hs), p['attn_k_norm_g']), pos)
    v = c_v.reshape(hs)
    L = k_past.shape[1]
    k_all = jnp.concatenate([k_past.astype(f32), k], axis=1)
    v_all = jnp.concatenate([v_past.astype(f32), v], axis=1)
    y_c = _dilated_attention(q, k_all, v_all, L + jnp.arange(T, dtype=jnp.int32)).reshape(B, T, C_W)
    y_d, S_d, shift_d = _rwkv7(d_p, shift_d.astype(f32), S_d.astype(f32), p)
    y_mix = jnp.concatenate([y_a, y_b, y_c, y_d], axis=-1)
    x = x + _mm(y_mix, p['w_out']).astype(x.dtype)
    h = _rmsnorm(x, p['ln_ffn_g'])
    hg, buf_f = _causal_dwconv(_mm(h, p['ffn_w_gate']), buf_f.astype(f32), p['ffn_conv_w'].astype(f32))
    hu = _mm(h, p['ffn_w_up'])
    x = x + _mm(jax.nn.silu(hg) * hu, p['ffn_w_down']).astype(x.dtype)
    return x, (k, v, S_a, h_b, buf_b, S_d, shift_d, buf_f)


def setup_inputs(seed: int = 0) -> dict:
    key = jax.random.key(seed)
    keys = jax.random.split(key, 48)
    counter = [0]
    f32 = jnp.float32

    def nxt():
        counter[0] += 1
        return keys[counter[0] - 1]

    def nrm(shape, scale):
        return jax.random.normal(nxt(), shape, f32) * scale

    def gain(shape):
        return 1.0 + nrm(shape, 0.05)

    def unif(shape, lo, hi):
        return jax.random.uniform(nxt(), shape, f32, lo, hi)

    c_keep = min(C_WIN, PAST_LEN)
    return {
        'x_prompt': nrm((BATCH, SEQ, D_MODEL), 1.0),
        'x_sample': nrm((DEC_BATCH, DEC_SEQ, D_MODEL), 1.0),
        'cache_attn_k': nrm((DEPTH, DEC_BATCH, c_keep, C_HEADS, C_HD), 1.0),
        'cache_attn_v': nrm((DEPTH, DEC_BATCH, c_keep, C_HEADS, C_HD), 1.0),
        'state_hgrn': nrm((DEPTH, DEC_BATCH, A_HEADS, A_HD, A_HD), 0.3),
        'state_lru_h': nrm((DEPTH, DEC_BATCH, B_W), 0.5),
        'state_lru_conv': nrm((DEPTH, DEC_BATCH, B_CONV - 1, B_W), 1.0),
        'state_rwkv': nrm((DEPTH, DEC_BATCH, D_HEADS, D_HD, D_HD), 0.3),
        'state_rwkv_shift': nrm((DEPTH, DEC_BATCH, D_SHIFT_W), 1.0),
        'state_ffn_conv': nrm((DEPTH, DEC_BATCH, FFN_CONV - 1, D_FF), 1.0),
        'ln_mix_g': gain((DEPTH, D_MODEL)),
        'w_in': nrm((DEPTH, D_MODEL, N_IN), D_MODEL ** -0.5),
        'hgrn_lb_logits': nrm((DEPTH, A_W), 0.5),
        'hgrn_norm_g': gain((DEPTH, A_HD)),
        'lru_conv_w': nrm((DEPTH, B_CONV, B_W), B_CONV ** -0.5),
        'lru_conv_b': nrm((DEPTH, B_W), 0.02),
        'lru_wa': nrm((DEPTH, B_HEADS, B_HD, B_HD), B_HD ** -0.5),
        'lru_ba': nrm((DEPTH, B_W), 0.1),
        'lru_wx': nrm((DEPTH, B_HEADS, B_HD, B_HD), B_HD ** -0.5),
        'lru_bx': nrm((DEPTH, B_W), 0.1),
        'lru_lambda': unif((DEPTH, B_W), 4.0, 9.0),
        'attn_q_norm_g': gain((DEPTH, C_HD)),
        'attn_k_norm_g': gain((DEPTH, C_HD)),
        'rwkv_mu': unif((DEPTH, D_SHIFT_W), 0.0, 1.0),
        'rwkv_w0': unif((DEPTH, D_W), -6.0, -1.0),
        'rwkv_w2': nrm((DEPTH, D_RANK_W, D_W), 0.5 * D_RANK_W ** -0.5),
        'rwkv_a0': nrm((DEPTH, D_W), 0.1),
        'rwkv_a2': nrm((DEPTH, D_RANK_A, D_W), D_RANK_A ** -0.5),
        'rwkv_g2': nrm((DEPTH, D_RANK_G, D_W), D_RANK_G ** -0.5),
        'rwkv_k_k': 0.85 + nrm((DEPTH, D_W), 0.05),
        'rwkv_k_a': gain((DEPTH, D_W)),
        'rwkv_r_k': nrm((DEPTH, D_W), 0.1),
        'rwkv_ln_g': gain((DEPTH, D_W)),
        'rwkv_ln_b': nrm((DEPTH, D_W), 0.01),
        'w_out': nrm((DEPTH, MIX_W, D_MODEL), 0.5 * MIX_W ** -0.5),
        'ln_ffn_g': gain((DEPTH, D_MODEL)),
        'ffn_w_gate': nrm((DEPTH, D_MODEL, D_FF), D_MODEL ** -0.5),
        'ffn_w_up': nrm((DEPTH, D_MODEL, D_FF), D_MODEL ** -0.5),
        'ffn_conv_w': nrm((DEPTH, FFN_CONV, D_FF), FFN_CONV ** -0.5),
        'ffn_w_down': nrm((DEPTH, D_FF, D_MODEL), 0.5 * D_FF ** -0.5),
    }


def reference(x_prompt, x_sample, cache_attn_k, cache_attn_v, state_hgrn, state_lru_h,
              state_lru_conv, state_rwkv, state_rwkv_shift, state_ffn_conv,
              ln_mix_g, w_in, hgrn_lb_logits, hgrn_norm_g, lru_conv_w, lru_conv_b,
              lru_wa, lru_ba, lru_wx, lru_bx, lru_lambda, attn_q_norm_g, attn_k_norm_g,
              rwkv_mu, rwkv_w0, rwkv_w2, rwkv_a0, rwkv_a2, rwkv_g2, rwkv_k_k, rwkv_k_a,
              rwkv_r_k, rwkv_ln_g, rwkv_ln_b, w_out, ln_ffn_g, ffn_w_gate, ffn_w_up,
              ffn_conv_w, ffn_w_down):
    f32 = jnp.float32
    B, T = x_prompt.shape[0], x_prompt.shape[1]
    TS = x_sample.shape[1]
    lb_soft = jax.nn.softmax(hgrn_lb_logits.astype(f32), axis=0)
    lb_all = jnp.cumsum(lb_soft, axis=0) - lb_soft[:1]
    pos_p = jnp.arange(T, dtype=jnp.int32)
    pos_s = PAST_LEN + jnp.arange(TS, dtype=jnp.int32)
    zero_state = (
        jnp.zeros((B, 0, C_HEADS, C_HD), f32),
        jnp.zeros((B, 0, C_HEADS, C_HD), f32),
        jnp.zeros((B, A_HEADS, A_HD, A_HD), f32),
        jnp.zeros((B, B_W), f32),
        jnp.zeros((B, B_CONV - 1, B_W), f32),
        jnp.zeros((B, D_HEADS, D_HD, D_HD), f32),
        jnp.zeros((B, D_SHIFT_W), f32),
        jnp.zeros((B, FFN_CONV - 1, D_FF), f32),
    )
    xp, xs = x_prompt, x_sample
    p_states, s_states = [], []
    for l in range(DEPTH):
        prm = {
            'ln_mix_g': ln_mix_g[l], 'w_in': w_in[l], 'hgrn_norm_g': hgrn_norm_g[l],
            'lru_conv_w': lru_conv_w[l], 'lru_conv_b': lru_conv_b[l], 'lru_wa': lru_wa[l],
            'lru_ba': lru_ba[l], 'lru_wx': lru_wx[l], 'lru_bx': lru_bx[l],
            'lru_lambda': lru_lambda[l], 'attn_q_norm_g': attn_q_norm_g[l],
            'attn_k_norm_g': attn_k_norm_g[l], 'rwkv_mu': rwkv_mu[l], 'rwkv_w0': rwkv_w0[l],
            'rwkv_w2': rwkv_w2[l], 'rwkv_a0': rwkv_a0[l], 'rwkv_a2': rwkv_a2[l],
            'rwkv_g2': rwkv_g2[l], 'rwkv_k_k': rwkv_k_k[l], 'rwkv_k_a': rwkv_k_a[l],
            'rwkv_r_k': rwkv_r_k[l], 'rwkv_ln_g': rwkv_ln_g[l], 'rwkv_ln_b': rwkv_ln_b[l],
            'w_out': w_out[l], 'ln_ffn_g': ln_ffn_g[l], 'ffn_w_gate': ffn_w_gate[l],
            'ffn_w_up': ffn_w_up[l], 'ffn_conv_w': ffn_conv_w[l], 'ffn_w_down': ffn_w_down[l],
        }
        xp, st_p = _layer(xp, pos_p, zero_state, lb_all[l], prm)
        p_states.append(st_p)
        s_in = (cache_attn_k[l], cache_attn_v[l], state_hgrn[l], state_lru_h[l], state_lru_conv[l],
                state_rwkv[l], state_rwkv_shift[l], state_ffn_conv[l])
        xs, st_s = _layer(xs, pos_s, s_in, lb_all[l], prm)
        s_states.append(st_s)
    n_keep = min(C_WIN, T)
    dp, ds = x_prompt.dtype, x_sample.dtype
    p_attn_k = jnp.stack([st[0][:, T - n_keep:] for st in p_states]).astype(dp)
    p_attn_v = jnp.stack([st[1][:, T - n_keep:] for st in p_states]).astype(dp)
    p_hgrn = jnp.stack([st[2] for st in p_states]).astype(dp)
    p_lru_h = jnp.stack([st[3] for st in p_states]).astype(dp)
    p_lru_conv = jnp.stack([st[4] for st in p_states]).astype(dp)
    p_rwkv = jnp.stack([st[5] for st in p_states]).astype(dp)
    p_rwkv_shift = jnp.stack([st[6] for st in p_states]).astype(dp)
    p_ffn_conv = jnp.stack([st[7] for st in p_states]).astype(dp)
    s_attn_k = jnp.stack([st[0] for st in s_states]).astype(ds)
    s_attn_v = jnp.stack([st[1] for st in s_states]).astype(ds)
    s_hgrn = jnp.stack([st[2] for st in s_states]).astype(ds)
    s_lru_h = jnp.stack([st[3] for st in s_states]).astype(ds)
    s_lru_conv = jnp.stack([st[4] for st in s_states]).astype(ds)
    s_rwkv = jnp.stack([st[5] for st in s_states]).astype(ds)
    s_rwkv_shift = jnp.stack([st[6] for st in s_states]).astype(ds)
    s_ffn_conv = jnp.stack([st[7] for st in s_states]).astype(ds)
    return (xp, xs, p_attn_k, p_attn_v, p_hgrn, p_lru_h, p_lru_conv, p_rwkv, p_rwkv_shift, p_ffn_conv,
            s_attn_k, s_attn_v, s_hgrn, s_lru_h, s_lru_conv, s_rwkv, s_rwkv_shift, s_ffn_conv)
```

```python
import functools
import math

import numpy as np
import jax
import jax.numpy as jnp
from jax import lax
from jax.experimental import pallas as pl
from jax.experimental.pallas import tpu as pltpu

F32 = jnp.float32
BF16 = jnp.bfloat16

PAST_LEN = 8192
A_HD = 128
B_HD = 128
C_HD = 128
D_HD = 64
ROPE_DIM = C_HD // 4
ROPE_THETA = 500000.0
C_PATTERNS = ((128, 1), (512, 4), (2048, 16))
LRU_C = 8.0
RMS_EPS = 1e-6
RWKV_GN_EPS = 64e-5
NEG_BIG = -30000.0
F_FLOOR = 1e-20

LANE = 128
SUBLANE = 8
VMEM_LIMIT_BYTES = 56 * 2**20

C_WIN = max(win for win, _ in C_PATTERNS)

TIME_BLOCK_CAP = 256
MM_TM_CAP = 1024
FFN_TN = 256
SCAN_STEPS_CAP = 32

HGRN_CHUNK = 16
PAD_T = 16
ATT_BLK = 128
MASK_BIAS = -1e30
HI = lax.Precision.HIGHEST


def _cp(*sem):
    return pltpu.CompilerParams(dimension_semantics=sem, vmem_limit_bytes=VMEM_LIMIT_BYTES)


def _pick(n, cap, quantum):
    if n <= cap:
        return n
    t = (cap // quantum) * quantum
    while t >= quantum:
        if n % t == 0:
            return t
        t -= quantum
    raise ValueError(f"no tile for {n} (cap {cap}, quantum {quantum})")


def _sigmoid(x):
    return jax.nn.sigmoid(x)


def _silu(x):
    return x * jax.nn.sigmoid(x)


def _softplus(x):
    return jnp.maximum(x, 0.0) + jnp.log(1.0 + jnp.exp(-jnp.abs(x)))


def _gelu_tanh(x):
    return x * (0.5 * (1.0 + jnp.tanh(math.sqrt(2.0 / math.pi) * (x + 0.044715 * (x * x * x)))))


def _norm_kernel(x_ref, g_ref, o_ref):
    x = x_ref[...]
    inv = lax.rsqrt(jnp.mean(x * x, axis=-1, keepdims=True) + RMS_EPS)
    o_ref[...] = (x * inv * g_ref[...]).astype(o_ref.dtype)


def _rmsnorm_bf16(x2, g):
    m, d = x2.shape
    tm = _pick(m, 256, SUBLANE)
    return pl.pallas_call(
        _norm_kernel,
        out_shape=jax.ShapeDtypeStruct((m, d), BF16),
        grid=(m // tm,),
        in_specs=[pl.BlockSpec((tm, d), lambda i: (i, 0)), pl.BlockSpec((1, d), lambda i: (0, 0))],
        out_specs=pl.BlockSpec((tm, d), lambda i: (i, 0)),
        compiler_params=_cp("parallel"),
        name="rmsnorm",
    )(x2, g.reshape(1, d).astype(F32))


def _mm_kernel(*refs, nk, has_res):
    if has_res:
        a_ref, w_ref, r_ref, o_ref = refs[:4]
    else:
        a_ref, w_ref, o_ref = refs[:3]
        r_ref = None
    part = jnp.dot(a_ref[...], w_ref[...], preferred_element_type=F32)
    if nk == 1:
        o_ref[...] = (r_ref[...] + part) if has_res else part
        return
    acc_ref = refs[-1]
    k = pl.program_id(2)

    @pl.when(k == 0)
    def _():
        acc_ref[...] = part

    @pl.when(k > 0)
    def _():
        acc_ref[...] += part

    @pl.when(k == nk - 1)
    def _():
        o_ref[...] = (r_ref[...] + acc_ref[...]) if has_res else acc_ref[...]


def _matmul(a, w, res=None, *, tm_cap, tn_cap, tk_cap, w_resident):
    m, k = a.shape
    n = w.shape[1]
    tm = _pick(m, tm_cap, SUBLANE)
    tn = _pick(n, tn_cap, LANE)
    tk = _pick(k, tk_cap, LANE)
    nk = k // tk
    if w_resident:
        grid = (n // tn, m // tm, nk)
        amap = lambda j, i, kk: (i, kk)
        wmap = lambda j, i, kk: (kk, j)
        omap = lambda j, i, kk: (i, j)
    else:
        grid = (m // tm, n // tn, nk)
        amap = lambda i, j, kk: (i, kk)
        wmap = lambda i, j, kk: (kk, j)
        omap = lambda i, j, kk: (i, j)
    in_specs = [pl.BlockSpec((tm, tk), amap), pl.BlockSpec((tk, tn), wmap)]
    args = [a, w]
    if res is not None:
        in_specs.append(pl.BlockSpec((tm, tn), omap))
        args.append(res)
    scratch = [pltpu.VMEM((tm, tn), F32)] if nk > 1 else []
    return pl.pallas_call(
        functools.partial(_mm_kernel, nk=nk, has_res=res is not None),
        out_shape=jax.ShapeDtypeStruct((m, n), F32),
        grid=grid,
        in_specs=in_specs,
        out_specs=pl.BlockSpec((tm, tn), omap),
        scratch_shapes=scratch,
        compiler_params=_cp("parallel", "parallel", "arbitrary"),
        name="matmul_res" if res is not None else "matmul",
    )(*args)


def _ffn_kernel(h_ref, wg_ref, wu_ref, cw_ref, st_ref, z_ref, so_ref, gs_ref, *, tm, i_state, r_state):
    i = pl.program_id(2)
    h = h_ref[...]
    gate = jnp.dot(h, wg_ref[...], preferred_element_type=F32)
    up = jnp.dot(h, wu_ref[...], preferred_element_type=F32)

    @pl.when(i == 0)
    def _():
        gs_ref[6:8, :] = st_ref[...]

    @pl.when(i > 0)
    def _():
        gs_ref[0:8, :] = gs_ref[tm:tm + 8, :]

    gs_ref[8:tm + 8, :] = gate
    cw = cw_ref[...]
    y = gs_ref[6:tm + 6, :] * cw[0:1, :]
    y = y + gs_ref[7:tm + 7, :] * cw[1:2, :]
    y = y + gate * cw[2:3, :]
    z_ref[...] = (_silu(y) * up).astype(z_ref.dtype)

    @pl.when(i == i_state)
    def _():
        so_ref[...] = gs_ref[8 + r_state:10 + r_state, :]


def _ffn_gate_up(h3, wg, wu, cw, st, n_valid):
    b, t, d = h3.shape
    f = wg.shape[1]
    tm = _pick(t, MM_TM_CAP, SUBLANE)
    tn = _pick(f, FFN_TN, LANE)
    i_state, r_state = divmod(n_valid - 2, tm)
    assert r_state + 2 <= tm
    return pl.pallas_call(
        functools.partial(_ffn_kernel, tm=tm, i_state=i_state, r_state=r_state),
        out_shape=(jax.ShapeDtypeStruct((b, t, f), BF16), jax.ShapeDtypeStruct((b, 2, f), F32)),
        grid=(f // tn, b, t // tm),
        in_specs=[
            pl.BlockSpec((None, tm, d), lambda j, bb, i: (bb, i, 0)),
            pl.BlockSpec((d, tn), lambda j, bb, i: (0, j)),
            pl.BlockSpec((d, tn), lambda j, bb, i: (0, j)),
            pl.BlockSpec((3, tn), lambda j, bb, i: (0, j)),
            pl.BlockSpec((None, 2, tn), lambda j, bb, i: (bb, 0, j)),
        ],
        out_specs=(
            pl.BlockSpec((None, tm, tn), lambda j, bb, i: (bb, i, j)),
            pl.BlockSpec((None, 2, tn), lambda j, bb, i: (bb, 0, j)),
        ),
        scratch_shapes=[pltpu.VMEM((tm + 8, tn), F32)],
        compiler_params=_cp("parallel", "parallel", "arbitrary"),
        name="ffn_gate_up",
    )(h3, wg, wu, cw, st)


def _hgrn_kernel(q_ref, f_ref, i_ref, g_ref, lb_ref, ng_ref, tri_ref, s0_ref, y_ref, so_ref,
                 st_ref, qs_ref, ks_ref, gc_ref, o_ref, *, tb, n_valid, masked, n_chunks):
    it = pl.program_id(2)
    ch = HGRN_CHUNK

    @pl.when(it == 0)
    def _():
        st_ref[...] = s0_ref[...]

    lb = lb_ref[...]
    fl = f_ref[...]
    f = lb + (1.0 - lb) * _sigmoid(fl)
    lf = jnp.log(jnp.maximum(f, F_FLOOR))
    kk = (1.0 - lb) * _sigmoid(-fl)
    if masked:
        row = it * tb + lax.broadcasted_iota(jnp.int32, (tb, 1), 0)
        live = row < n_valid
        lf = jnp.where(live, lf, 0.0)
        kk = jnp.where(live, kk, 0.0)
    qs_ref[...] = _silu(q_ref[...]) * (A_HD ** -0.5)
    ks_ref[...] = kk
    gc_ref[...] = jnp.dot(tri_ref[...], lf, precision=HI, preferred_element_type=F32)

    rows = lax.broadcasted_iota(jnp.int32, (ch, 1), 0)

    def chunk(c, st):
        base = pl.multiple_of(c * ch, ch)
        sl = pl.ds(base, ch)
        q = qs_ref[sl, :]
        k = ks_ref[sl, :]
        g = gc_ref[sl, :]
        v = i_ref[sl, :]
        o = lax.dot_general((q * jnp.exp(g)).astype(BF16), st.astype(BF16),
                            (((1,), (1,)), ((), ())), preferred_element_type=F32)
        for s in range(ch):
            gs = gc_ref[pl.ds(base + s, 1), :]
            ksr = ks_ref[pl.ds(base + s, 1), :]
            vs = i_ref[pl.ds(base + s, 1), :]
            dec = jnp.exp(jnp.where(rows >= s, g - gs, NEG_BIG))
            a = jnp.sum(q * ksr * dec, axis=-1, keepdims=True)
            o = o + a * vs
        o_ref[sl, :] = o
        gl = gc_ref[pl.ds(base + ch - 1, 1), :]
        kt = k * jnp.exp(gl - g)
        upd = lax.dot_general(v.astype(BF16), kt.astype(BF16),
                              (((0,), (0,)), ((), ())), preferred_element_type=F32)
        return st * jnp.exp(gl) + upd

    st = lax.fori_loop(0, n_chunks, chunk, st_ref[...])
    st_ref[...] = st
    so_ref[...] = st
    o = o_ref[...]
    o = o * lax.rsqrt(jnp.mean(o * o, axis=-1, keepdims=True) + RMS_EPS) * ng_ref[...]
    y_ref[...] = (o * _silu(g_ref[...])).astype(y_ref.dtype)


def _hgrn(proj3, lb, norm_g, s0_t, w, n_valid):
    b, t, _ = proj3.shape
    nh = w // A_HD
    tb = _pick(t, TIME_BLOCK_CAP, HGRN_CHUNK)
    ch = HGRN_CHUNK
    idx = np.arange(tb)
    tri = ((idx[:, None] // ch == idx[None, :] // ch) & (idx[None, :] <= idx[:, None])).astype(np.float32)

    def col(off):
        return lambda bb, h, i: (bb, i, off * nh + h)

    return pl.pallas_call(
        functools.partial(_hgrn_kernel, tb=tb, n_valid=n_valid, masked=n_valid < t, n_chunks=tb // ch),
        out_shape=(jax.ShapeDtypeStruct((b, t, w), BF16), jax.ShapeDtypeStruct((b, nh, A_HD, A_HD), F32)),
        grid=(b, nh, t // tb),
        in_specs=[
            pl.BlockSpec((None, tb, A_HD), col(0)),
            pl.BlockSpec((None, tb, A_HD), col(1)),
            pl.BlockSpec((None, tb, A_HD), col(2)),
            pl.BlockSpec((None, tb, A_HD), col(3)),
            pl.BlockSpec((1, A_HD), lambda bb, h, i: (0, h)),
            pl.BlockSpec((1, A_HD), lambda bb, h, i: (0, 0)),
            pl.BlockSpec((tb, tb), lambda bb, h, i: (0, 0)),
            pl.BlockSpec((None, None, A_HD, A_HD), lambda bb, h, i: (bb, h, 0, 0)),
        ],
        out_specs=(
            pl.BlockSpec((None, tb, A_HD), lambda bb, h, i: (bb, i, h)),
            pl.BlockSpec((None, None, A_HD, A_HD), lambda bb, h, i: (bb, h, 0, 0)),
        ),
        scratch_shapes=[pltpu.VMEM((A_HD, A_HD), F32)] + [pltpu.VMEM((tb, A_HD), F32)] * 4,
        compiler_params=_cp("parallel", "parallel", "arbitrary"),
        name="hgrn2",
    )(proj3, proj3, proj3, proj3, lb.reshape(1, w), norm_g.reshape(1, A_HD), jnp.asarray(tri), s0_t)


def _lru_kernel(x_ref, g_ref, buf_ref, h0_ref, cw_ref, cb_ref, wa_ref, ba_ref, wx_ref, bx_ref, lam_ref,
                y_ref, hl_ref, bo_ref, xs_ref, a_ref, b_ref, hc_ref, *, tb, n_valid, masked, i_state, r_state):
    it = pl.program_id(1)
    nh = wa_ref.shape[0]

    @pl.when(it == 0)
    def _():
        xs_ref[5:8, :] = buf_ref[...]
        hc_ref[...] = h0_ref[...]

    @pl.when(it > 0)
    def _():
        xs_ref[0:8, :] = xs_ref[tb:tb + 8, :]

    x = x_ref[...]
    xs_ref[8:tb + 8, :] = x
    cw = cw_ref[...]
    xc = xs_ref[5:tb + 5, :] * cw[0:1, :]
    xc = xc + xs_ref[6:tb + 6, :] * cw[1:2, :]
    xc = xc + xs_ref[7:tb + 7, :] * cw[2:3, :]
    xc = xc + x * cw[3:4, :]
    xc = xc + cb_ref[...]
    xcb = xc.astype(BF16)
    ra, rx = [], []
    for h in range(nh):
        xh = xcb[:, h * B_HD:(h + 1) * B_HD]
        ra.append(jnp.dot(xh, wa_ref[h], preferred_element_type=F32))
        rx.append(jnp.dot(xh, wx_ref[h], preferred_element_type=F32))
    r = _sigmoid(jnp.concatenate(ra, axis=1) + ba_ref[...])
    ig = _sigmoid(jnp.concatenate(rx, axis=1) + bx_ref[...])
    log_a = (-LRU_C) * r * _softplus(-lam_ref[...])
    a = jnp.exp(log_a)
    om = -jnp.tanh(log_a) * (a * a + 1.0)
    bb = jnp.sqrt(jnp.maximum(om, 0.0)) * (ig * xc)
    if masked:
        row = it * tb + lax.broadcasted_iota(jnp.int32, (tb, 1), 0)
        live = row < n_valid
        a = jnp.where(live, a, 1.0)
        bb = jnp.where(live, bb, 0.0)
    a_ref[...] = a
    b_ref[...] = bb

    def step(t, h):
        h = a_ref[pl.ds(t, 1), :] * h + b_ref[pl.ds(t, 1), :]
        b_ref[pl.ds(t, 1), :] = h
        return h

    h = lax.fori_loop(0, tb, step, hc_ref[...], unroll=8)
    hc_ref[...] = h
    hl_ref[...] = h
    y_ref[...] = (b_ref[...] * _gelu_tanh(g_ref[...])).astype(y_ref.dtype)

    @pl.when(it == i_state)
    def _():
        bo_ref[...] = xs_ref[8 + r_state:11 + r_state, :]


def _lru(proj3, buf, h0, p, w, n_valid):
    b, t, _ = proj3.shape
    nh = w // B_HD
    tb = _pick(t, TIME_BLOCK_CAP, SUBLANE)
    i_state, r_state = divmod(n_valid - 3, tb)
    assert n_valid >= 3 and r_state + 3 <= tb
    row = lambda v: v.reshape(1, w).astype(F32)
    full = lambda shape: pl.BlockSpec(shape, lambda bb, i: (0,) * len(shape))
    return pl.pallas_call(
        functools.partial(_lru_kernel, tb=tb, n_valid=n_valid, masked=n_valid < t,
                          i_state=i_state, r_state=r_state),
        out_shape=(jax.ShapeDtypeStruct((b, t, w), BF16), jax.ShapeDtypeStruct((b, 1, w), F32),
                   jax.ShapeDtypeStruct((b, 3, w), F32)),
        grid=(b, t // tb),
        in_specs=[
            pl.BlockSpec((None, tb, w), lambda bb, i: (bb, i, 4)),
            pl.BlockSpec((None, tb, w), lambda bb, i: (bb, i, 5)),
            pl.BlockSpec((None, 3, w), lambda bb, i: (bb, 0, 0)),
            pl.BlockSpec((None, 1, w), lambda bb, i: (bb, 0, 0)),
            full((4, w)), full((1, w)), full((nh, B_HD, B_HD)), full((1, w)),
            full((nh, B_HD, B_HD)), full((1, w)), full((1, w)),
        ],
        out_specs=(
            pl.BlockSpec((None, tb, w), lambda bb, i: (bb, i, 0)),
            pl.BlockSpec((None, 1, w), lambda bb, i: (bb, 0, 0)),
            pl.BlockSpec((None, 3, w), lambda bb, i: (bb, 0, 0)),
        ),
        scratch_shapes=[pltpu.VMEM((tb + 8, w), F32), pltpu.VMEM((tb, w), F32), pltpu.VMEM((tb, w), F32),
                        pltpu.VMEM((1, w), F32)],
        compiler_params=_cp("parallel", "arbitrary"),
        name="rglru",
    )(proj3, proj3, buf, h0.reshape(b, 1, w), p['lru_conv_w'].astype(F32), row(p['lru_conv_b']),
      p['lru_wa'].astype(BF16), row(p['lru_ba']), p['lru_wx'].astype(BF16), row(p['lru_bx']),
      row(p['lru_lambda']))


def _qk_kernel(q_ref, k_ref, qg_ref, kg_ref, c_ref, sa_ref, sb_ref, qo_ref, ko_ref):
    nh = q_ref.shape[1] // C_HD
    cosf, sina, sinb = c_ref[...], sa_ref[...], sb_ref[...]

    def prep(x, g):
        xn = x * lax.rsqrt(jnp.mean(x * x, axis=-1, keepdims=True) + RMS_EPS) * g
        up = pltpu.roll(xn, C_HD - ROPE_DIM // 2, axis=1)
        dn = pltpu.roll(xn, ROPE_DIM // 2, axis=1)
        return xn * cosf + up * sina + dn * sinb

    for h in range(nh):
        sl = slice(h * C_HD, (h + 1) * C_HD)
        qo_ref[:, sl] = (prep(q_ref[:, sl], qg_ref[...]) * (C_HD ** -0.5)).astype(qo_ref.dtype)
        ko_ref[:, sl] = prep(k_ref[:, sl], kg_ref[...])


def _rope_tables(pos):
    half = ROPE_DIM // 2
    inv = ROPE_THETA ** (-jnp.arange(half, dtype=F32) / half)
    ang = pos.astype(F32)[:, None] * inv[None, :]
    cos, sin = jnp.cos(ang), jnp.sin(ang)
    n = pos.shape[0]
    rest = C_HD - ROPE_DIM
    cosf = jnp.concatenate([cos, cos, jnp.ones((n, rest), F32)], axis=1)
    sina = jnp.concatenate([-sin, jnp.zeros((n, half + rest), F32)], axis=1)
    sinb = jnp.concatenate([jnp.zeros((n, half), F32), sin, jnp.zeros((n, rest), F32)], axis=1)
    return cosf, sina, sinb


def _qk_prep(proj3, qg, kg, tables, w):
    b, t, _ = proj3.shape
    tb = _pick(t, TIME_BLOCK_CAP, SUBLANE)
    tab = pl.BlockSpec((tb, C_HD), lambda bb, i: (i, 0))
    gsp = pl.BlockSpec((1, C_HD), lambda bb, i: (0, 0))
    return pl.pallas_call(
        _qk_kernel,
        out_shape=(jax.ShapeDtypeStruct((b, t, w), BF16), jax.ShapeDtypeStruct((b, t, w), F32)),
        grid=(b, t // tb),
        in_specs=[pl.BlockSpec((None, tb, w), lambda bb, i: (bb, i, 6)),
                  pl.BlockSpec((None, tb, w), lambda bb, i: (bb, i, 7)), gsp, gsp, tab, tab, tab],
        out_specs=(pl.BlockSpec((None, tb, w), lambda bb, i: (bb, i, 0)),
                   pl.BlockSpec((None, tb, w), lambda bb, i: (bb, i, 0))),
        compiler_params=_cp("parallel", "parallel"),
        name="qk_norm_rope",
    )(proj3, proj3, qg.reshape(1, C_HD).astype(F32), kg.reshape(1, C_HD).astype(F32), *tables)


def _pattern_bias(dist):
    dist = np.asarray(dist, np.int64)
    cnt = np.zeros(dist.shape, np.int64)
    for win, dil in C_PATTERNS:
        cnt += (dist >= 0) & (dist <= win) & (dist % dil == 0)
    return np.where(cnt > 0, np.log(np.maximum(cnt, 1)), MASK_BIAS).astype(np.float32)


def _attn_kernel(q_ref, k_ref, v_ref, bias_ref, o_ref, *, n_off):
    qi = pl.program_id(2)
    q = q_ref[...]
    blk = ATT_BLK

    def body(off, carry):
        m, l, acc = carry
        ks = pl.ds(pl.multiple_of((qi - off) * blk, blk), blk)
        kb = k_ref[ks, :].astype(BF16)
        vb = v_ref[ks, :].astype(BF16)
        s = lax.dot_general(q, kb, (((1,), (1,)), ((), ())), preferred_element_type=F32) + bias_ref[off]
        mn = jnp.maximum(m, jnp.max(s, axis=-1, keepdims=True))
        alpha = jnp.exp(m - mn)
        p = jnp.exp(s - mn)
        l = alpha * l + jnp.sum(p, axis=-1, keepdims=True)
        acc = alpha * acc + jnp.dot(p.astype(BF16), vb, preferred_element_type=F32)
        return mn, l, acc

    init = (jnp.full((blk, 1), MASK_BIAS, F32), jnp.zeros((blk, 1), F32), jnp.zeros((blk, C_HD), F32))
    _, l, acc = lax.fori_loop(0, jnp.minimum(qi, n_off - 1) + 1, body, init)
    o_ref[...] = (acc / l).astype(o_ref.dtype)


def _attn_prompt(qn, kn, proj3, w):
    b, t, _ = qn.shape
    nh = w // C_HD
    blk = ATT_BLK
    assert t % blk == 0
    n_off = min(-(-C_WIN // blk) + 1, t // blk)
    r = np.arange(blk)
    bias = np.stack([_pattern_bias(off * blk + r[:, None] - r[None, :]) for off in range(n_off)])
    vcol = 8 * nh
    return pl.pallas_call(
        functools.partial(_attn_kernel, n_off=n_off),
        out_shape=jax.ShapeDtypeStruct((b, t, w), BF16),
        grid=(b, nh, t // blk),
        in_specs=[
            pl.BlockSpec((None, blk, C_HD), lambda bb, h, i: (bb, i, h)),
            pl.BlockSpec((None, t, C_HD), lambda bb, h, i: (bb, 0, h)),
            pl.BlockSpec((None, t, C_HD), lambda bb, h, i: (bb, 0, vcol + h)),
            pl.BlockSpec((n_off, blk, blk), lambda bb, h, i: (0, 0, 0)),
        ],
        out_specs=pl.BlockSpec((None, blk, C_HD), lambda bb, h, i: (bb, i, h)),
        compiler_params=_cp("parallel", "parallel", "arbitrary"),
        name="attn_prompt",
    )(qn, kn, proj3, jnp.asarray(bias))


def _attn_s_kernel(q_ref, kp_ref, vp_ref, kn_ref, vn_ref, bp_ref, bn_ref, o_ref):
    q = q_ref[...]
    nt = (((1,), (1,)), ((), ()))
    sp = lax.dot_general(q, kp_ref[...].astype(BF16), nt, preferred_element_type=F32) + bp_ref[...]
    sn = lax.dot_general(q, kn_ref[...].astype(BF16), nt, preferred_element_type=F32) + bn_ref[...]
    m = jnp.maximum(jnp.max(sp, axis=-1, keepdims=True), jnp.max(sn, axis=-1, keepdims=True))
    pp = jnp.exp(sp - m)
    pn = jnp.exp(sn - m)
    l = jnp.sum(pp, axis=-1, keepdims=True) + jnp.sum(pn, axis=-1, keepdims=True)
    acc = jnp.dot(pp.astype(BF16), vp_ref[...].astype(BF16), preferred_element_type=F32)
    acc = acc + jnp.dot(pn.astype(BF16), vn_ref[...].astype(BF16), preferred_element_type=F32)
    o_ref[...] = (acc / l).astype(o_ref.dtype)


def _attn_sample(qn, kpast, vpast, kn, proj3, w, n_valid):
    b, tp, _ = qn.shape
    lp = kpast.shape[1]
    nh = w // C_HD
    t = np.arange(tp)
    live = t < n_valid
    bp = _pattern_bias(lp + t[:, None] - np.arange(lp)[None, :])
    bp = np.where(live[:, None], bp, 0.0).astype(np.float32)
    bn = _pattern_bias(t[:, None] - t[None, :])
    bn = np.where(live[:, None] & live[None, :], bn, MASK_BIAS).astype(np.float32)
    vcol = 8 * nh
    new = pl.BlockSpec((None, tp, C_HD), lambda bb, h: (bb, 0, h))
    past = pl.BlockSpec((None, lp, C_HD), lambda bb, h: (bb, 0, h))
    return pl.pallas_call(
        _attn_s_kernel,
        out_shape=jax.ShapeDtypeStruct((b, tp, w), BF16),
        grid=(b, nh),
        in_specs=[new, past, past, new,
                  pl.BlockSpec((None, tp, C_HD), lambda bb, h: (bb, 0, vcol + h)),
                  pl.BlockSpec((tp, lp), lambda bb, h: (0, 0)),
                  pl.BlockSpec((tp, tp), lambda bb, h: (0, 0))],
        out_specs=new,
        compiler_params=_cp("parallel", "parallel"),
        name="attn_sample",
    )(qn, kpast, vpast, kn, proj3, jnp.asarray(bp), jnp.asarray(bn))


def _seg_sum(x, e_ref):
    nb = x.shape[1] // LANE
    e = e_ref[...]
    return jnp.concatenate(
        [jnp.dot(x[:, i * LANE:(i + 1) * LANE], e, precision=HI, preferred_element_type=F32) for i in range(nb)],
        axis=1)


def _rwkv_prep_kernel(r_ref, k_ref, v_ref, x_ref, sr_ref, sk_ref, sv_ref, sx_ref,
                      mr_ref, mk_ref, mv_ref, mx_ref, w0_ref, w2_ref, a0_ref, a2_ref, g2_ref,
                      kk_ref, ka_ref, e_ref,
                      ro_ref, wo_ref, ko_ref, vo_ref, zo_ref, bo_ref, go_ref,
                      pr_ref, pk_ref, pv_ref, px_ref, *, tb, n_valid, masked):
    it = pl.program_id(1)

    @pl.when(it == 0)
    def _():
        pr_ref[...] = sr_ref[...]
        pk_ref[...] = sk_ref[...]
        pv_ref[...] = sv_ref[...]
        px_ref[...] = sx_ref[...]

    first = lax.broadcasted_iota(jnp.int32, (tb, 1), 0) == 0

    def shifted(x_ref_, prev_ref, mu_ref):
        x = x_ref_[...]
        prev = jnp.where(first, prev_ref[...], pltpu.roll(x, 1, axis=0))
        prev_ref[...] = x[tb - 1:tb, :]
        return x + mu_ref[...] * (prev - x)

    r = shifted(r_ref, pr_ref, mr_ref)
    k = shifted(k_ref, pk_ref, mk_ref)
    v = shifted(v_ref, pv_ref, mv_ref)
    x = shifted(x_ref, px_ref, mx_ref)
    lo = x[:, 0:LANE]
    hi = x[:, LANE:]
    mw = jnp.dot(jnp.tanh(lo).astype(BF16), w2_ref[...], preferred_element_type=F32)
    ma = jnp.dot(lo.astype(BF16), a2_ref[...], preferred_element_type=F32)
    g = jnp.dot(_sigmoid(hi).astype(BF16), g2_ref[...], preferred_element_type=F32)
    log_w = -_softplus(-(w0_ref[...] + mw)) - 0.5
    decay = jnp.exp(-jnp.exp(log_w))
    a = _sigmoid(a0_ref[...] + ma)
    kk = k * kk_ref[...]
    kk = kk / jnp.maximum(jnp.sqrt(_seg_sum(kk * kk, e_ref)), 1e-12)
    k2 = k * (1.0 + (a - 1.0) * ka_ref[...])
    bvec = kk * a
    if masked:
        live = (it * tb + lax.broadcasted_iota(jnp.int32, (tb, 1), 0)) < n_valid
        decay = jnp.where(live, decay, 1.0)
        k2 = jnp.where(live, k2, 0.0)
        bvec = jnp.where(live, bvec, 0.0)
    ro_ref[...] = r
    wo_ref[...] = decay
    ko_ref[...] = k2
    vo_ref[...] = v
    zo_ref[...] = -kk
    bo_ref[...] = bvec
    go_ref[...] = g


def _rwkv_prep(proj3, shift, pw, w, xw, n_valid):
    b, t, _ = proj3.shape
    tb = _pick(t, TIME_BLOCK_CAP, SUBLANE)
    xcol = (12 * w) // xw
    big = lambda c: pl.BlockSpec((None, tb, w), lambda bb, i: (bb, i, c))
    st = lambda n: pl.BlockSpec((None, 1, n), lambda bb, i: (bb, 0, 0))
    full = lambda shape: pl.BlockSpec(shape, lambda bb, i: (0,) * len(shape))
    out = pl.BlockSpec((None, tb, w), lambda bb, i: (bb, i, 0))
    sr, sk, sv, sx = shift
    return pl.pallas_call(
        functools.partial(_rwkv_prep_kernel, tb=tb, n_valid=n_valid, masked=n_valid < t),
        out_shape=tuple(jax.ShapeDtypeStruct((b, t, w), F32) for _ in range(7)),
        grid=(b, t // tb),
        in_specs=[big(9), big(10), big(11), pl.BlockSpec((None, tb, xw), lambda bb, i: (bb, i, xcol)),
                  st(w), st(w), st(w), st(xw),
                  full((1, w)), full((1, w)), full((1, w)), full((1, xw)),
                  full((1, w)), full((LANE, w)), full((1, w)), full((LANE, w)), full((xw - LANE, w)),
                  full((1, w)), full((1, w)), full((LANE, LANE))],
        out_specs=tuple(out for _ in range(7)),
        scratch_shapes=[pltpu.VMEM((1, w), F32)] * 3 + [pltpu.VMEM((1, xw), F32)],
        compiler_params=_cp("parallel", "arbitrary"),
        name="rwkv_prep",
    )(proj3, proj3, proj3, proj3, sr, sk, sv, sx, *pw)


def _rwkv_scan_kernel(r_ref, w_ref, k_ref, z_ref, b_ref, v_ref, s0_ref, o_ref, so_ref, s_ref, *, tc):
    nk = s_ref.shape[0]

    @pl.when(pl.program_id(0) == 0)
    def _():
        s_ref[...] = s0_ref[...]

    def tree(parts):
        while len(parts) > 1:
            parts = [parts[i] + parts[i + 1] for i in range(0, len(parts), 2)]
        return parts[0]

    def step(t, carry):
        z = z_ref[t]
        acc = [None] * 4
        for kx in range(nk):
            term = s_ref[kx] * z[kx:kx + 1, :]
            acc[kx % 4] = term if acc[kx % 4] is None else acc[kx % 4] + term
        sz = tree(acc)
        wv, bv, kv, rv, vv = w_ref[t], b_ref[t], k_ref[t], r_ref[t], v_ref[t]
        acc = [None] * 4
        for kx in range(nk):
            sl = slice(kx, kx + 1)
            sn = s_ref[kx] * wv[sl, :] + sz * bv[sl, :] + vv * kv[sl, :]
            s_ref[kx] = sn
            term = sn * rv[sl, :]
            acc[kx % 4] = term if acc[kx % 4] is None else acc[kx % 4] + term
        o_ref[t] = tree(acc)
        return carry

    lax.fori_loop(0, tc, step, 0)
    so_ref[...] = s_ref[...]


def _rwkv_scan(rl, wl, kl, zl, bl, vl, s0l):
    t, nk, _ = rl.shape
    nv = vl.shape[1]
    tc = _pick(t, SCAN_STEPS_CAP, 1)
    tok = pl.BlockSpec((tc, nk, LANE), lambda i: (i, 0, 0))
    vsp = pl.BlockSpec((tc, nv, LANE), lambda i: (i, 0, 0))
    ssp = pl.BlockSpec((nk, nv, LANE), lambda i: (0, 0, 0))
    return pl.pallas_call(
        functools.partial(_rwkv_scan_kernel, tc=tc),
        out_shape=(jax.ShapeDtypeStruct((t, nv, LANE), F32), jax.ShapeDtypeStruct((nk, nv, LANE), F32)),
        grid=(t // tc,),
        in_specs=[tok, tok, tok, tok, tok, vsp, ssp],
        out_specs=(vsp, ssp),
        scratch_shapes=[pltpu.VMEM((nk, nv, LANE), F32)],
        compiler_params=_cp("arbitrary"),
        name="rwkv_scan",
    )(rl, wl, kl, zl, bl, vl, s0l)


def _rwkv_post_kernel(o_ref, r_ref, k_ref, v_ref, g_ref, lg_ref, lb_ref, rk_ref, e_ref, y_ref):
    o = o_ref[...]
    inv_n = 1.0 / D_HD
    mu = _seg_sum(o, e_ref) * inv_n
    d = o - mu
    var = _seg_sum(d * d, e_ref) * inv_n
    on = d * lax.rsqrt(var + RWKV_GN_EPS) * lg_ref[...] + lb_ref[...]
    bonus = _seg_sum(r_ref[...] * k_ref[...] * rk_ref[...], e_ref) * v_ref[...]
    y_ref[...] = ((on + bonus) * g_ref[...]).astype(y_ref.dtype)


def _rwkv_post(o3, r3, k3, v3, g3, lg, lb, rk, e):
    b, t, w = o3.shape
    tb = _pick(t, TIME_BLOCK_CAP, SUBLANE)
    big = pl.BlockSpec((None, tb, w), lambda bb, i: (bb, i, 0))
    rowsp = pl.BlockSpec((1, w), lambda bb, i: (0, 0))
    return pl.pallas_call(
        _rwkv_post_kernel,
        out_shape=jax.ShapeDtypeStruct((b, t, w), BF16),
        grid=(b, t // tb),
        in_specs=[big] * 5 + [rowsp] * 3 + [pl.BlockSpec((LANE, LANE), lambda bb, i: (0, 0))],
        out_specs=big,
        compiler_params=_cp("parallel", "parallel"),
        name="rwkv_post",
    )(o3, r3, k3, v3, g3, lg, lb, rk, e)


def _rwkv(proj3, shift, s0, pw, w, xw, n_valid):
    b, t, _ = proj3.shape
    nh = w // D_HD
    chains = b * nh
    assert LANE % chains == 0 and D_HD % (LANE // chains) == 0
    vh = LANE // chains
    nv = D_HD // vh
    prep_w, (lg, lb, rk, e) = pw
    r3, w3, k3, v3, z3, b3, g3 = _rwkv_prep(proj3, shift, prep_w, w, xw, n_valid)

    def klay(x3):
        x = x3.reshape(b, t, nh, D_HD).transpose(1, 3, 0, 2).reshape(t, D_HD, chains)
        return jnp.tile(x, (1, 1, vh))

    def vlay(x3):
        x = x3.reshape(b, t, nh, vh, nv).transpose(1, 4, 3, 0, 2)
        return x.reshape(t, nv, LANE)

    s0l = s0.reshape(b, nh, vh, nv, D_HD).transpose(4, 3, 2, 0, 1).reshape(D_HD, nv, LANE)
    ol, sl = _rwkv_scan(klay(r3), klay(w3), klay(k3), klay(z3), klay(b3), vlay(v3), s0l)
    o3 = ol.reshape(t, nv, vh, b, nh).transpose(3, 0, 4, 2, 1).reshape(b, t, w)
    s_new = sl.reshape(D_HD, nv, vh, b, nh).transpose(3, 4, 2, 1, 0).reshape(b, nh, D_HD, D_HD)
    y = _rwkv_post(o3, r3, k3, v3, g3, lg, lb, rk, e)
    return y, s_new


def _layer(x3, n_valid, tables, state, lbv, p, dims):
    d, w, xw, n_in = dims
    b, t, _ = x3.shape
    k_past, v_past, s_a, h_b, buf_b, s_d, shift_d, buf_f = state
    m = b * t
    x2 = x3.reshape(m, d)
    h = _rmsnorm_bf16(x2, p['ln_mix_g'])
    proj = _matmul(h, p['w_in'], tm_cap=MM_TM_CAP, tn_cap=1152, tk_cap=d, w_resident=True)
    proj3 = proj.reshape(b, t, -1)
    y_a, s_a_t = _hgrn(proj3, lbv, p['hgrn_norm_g'].astype(F32), jnp.swapaxes(s_a, -1, -2), w, n_valid)
    s_a_new = jnp.swapaxes(s_a_t, -1, -2)
    y_b, h_b_new, buf_b_new = _lru(proj3, buf_b, h_b, p, w, n_valid)
    qn, kn = _qk_prep(proj3, p['attn_q_norm_g'], p['attn_k_norm_g'], tables, w)
    if k_past is None:
        y_c = _attn_prompt(qn, kn, proj3, w)
    else:
        lp = k_past.shape[1]
        y_c = _attn_sample(qn, k_past.reshape(b, lp, w), v_past.reshape(b, lp, w), kn, proj3, w, n_valid)
    v_new = proj3[:, :, 8 * w:9 * w]
    sh = shift_d.reshape(b, 1, -1)
    shift = (sh[:, :, 0:w], sh[:, :, w:2 * w], sh[:, :, 2 * w:3 * w],
             jnp.pad(sh[:, :, 3 * w:], ((0, 0), (0, 0), (0, xw - (sh.shape[2] - 3 * w)))))
    y_d, s_d_new = _rwkv(proj3, shift, s_d, p['rwkv'], w, xw, n_valid)
    shift_new = proj3[:, n_valid - 1, 9 * w:n_in]
    y_mix = jnp.concatenate([y_a, y_b, y_c, y_d], axis=-1).reshape(m, 4 * w)
    x2 = _matmul(y_mix, p['w_out'], x2, tm_cap=MM_TM_CAP, tn_cap=512, tk_cap=4 * w, w_resident=True)
    h = _rmsnorm_bf16(x2, p['ln_ffn_g'])
    z, buf_f_new = _ffn_gate_up(h.reshape(b, t, d), p['ffn_w_gate'], p['ffn_w_up'], p['ffn_conv_w'], buf_f, n_valid)
    f = z.shape[-1]
    x2 = _matmul(z.reshape(m, f), p['ffn_w_down'], x2, tm_cap=MM_TM_CAP, tn_cap=512, tk_cap=f // 2,
                 w_resident=False)
    keep = slice(max(n_valid - C_WIN, 0), n_valid) if k_past is None else slice(0, n_valid)
    new_state = (kn[:, keep], v_new[:, keep], s_a_new, h_b_new.reshape(b, w), buf_b_new,
                 s_d_new, shift_new, buf_f_new)
    return x2.reshape(b, t, d), new_state


def kernel(x_prompt, x_sample, cache_attn_k, cache_attn_v, state_hgrn, state_lru_h, state_lru_conv, state_rwkv, state_rwkv_shift, state_ffn_conv, ln_mix_g, w_in, hgrn_lb_logits, hgrn_norm_g, lru_conv_w, lru_conv_b, lru_wa, lru_ba, lru_wx, lru_bx, lru_lambda, attn_q_norm_g, attn_k_norm_g, rwkv_mu, rwkv_w0, rwkv_w2, rwkv_a0, rwkv_a2, rwkv_g2, rwkv_k_k, rwkv_k_a, rwkv_r_k, rwkv_ln_g, rwkv_ln_b, w_out, ln_ffn_g, ffn_w_gate, ffn_w_up, ffn_conv_w, ffn_w_down):
    bp, tp, d = x_prompt.shape
    bs, ts, _ = x_sample.shape
    depth = w_in.shape[0]
    n_in = w_in.shape[2]
    w = d // 4
    n_pad = -(-n_in // LANE) * LANE
    rank_w, rank_a, rank_g = rwkv_w2.shape[1], rwkv_a2.shape[1], rwkv_g2.shape[1]
    assert rank_w + rank_a == LANE
    xw = n_pad - 12 * w
    dims = (d, w, xw, n_in)
    row = lambda v: v.reshape(1, -1).astype(F32)

    lb_soft = jax.nn.softmax(hgrn_lb_logits.astype(F32), axis=0)
    lb_all = jnp.cumsum(lb_soft, axis=0) - lb_soft[:1]

    w_in_b = jnp.pad(w_in, ((0, 0), (0, 0), (0, n_pad - n_in))).astype(BF16)
    w_out_b = w_out.astype(BF16)
    wg_b, wu_b, wd_b = ffn_w_gate.astype(BF16), ffn_w_up.astype(BF16), ffn_w_down.astype(BF16)
    zeros = lambda r: jnp.zeros((r, w), F32)
    head_id = np.arange(LANE) // D_HD
    e_blk = jnp.asarray((head_id[:, None] == head_id[None, :]).astype(np.float32))

    layers = []
    for l in range(depth):
        mu = rwkv_mu[l].astype(F32)
        prep_w = (
            row(mu[0:w]), row(mu[w:2 * w]), row(mu[2 * w:3 * w]),
            row(jnp.pad(mu[3 * w:], (0, xw - (mu.shape[0] - 3 * w)))),
            row(rwkv_w0[l]),
            jnp.concatenate([rwkv_w2[l].astype(F32), zeros(rank_a)], axis=0).astype(BF16),
            row(rwkv_a0[l]),
            jnp.concatenate([zeros(rank_w), rwkv_a2[l].astype(F32)], axis=0).astype(BF16),
            jnp.concatenate([rwkv_g2[l].astype(F32), zeros(xw - LANE - rank_g)], axis=0).astype(BF16),
            row(rwkv_k_k[l]), row(rwkv_k_a[l]), e_blk,
        )
        post_w = (row(rwkv_ln_g[l]), row(rwkv_ln_b[l]), row(rwkv_r_k[l]), e_blk)
        layers.append({
            'ln_mix_g': ln_mix_g[l], 'w_in': w_in_b[l], 'hgrn_norm_g': hgrn_norm_g[l],
            'lru_conv_w': lru_conv_w[l], 'lru_conv_b': lru_conv_b[l], 'lru_wa': lru_wa[l],
            'lru_ba': lru_ba[l], 'lru_wx': lru_wx[l], 'lru_bx': lru_bx[l], 'lru_lambda': lru_lambda[l],
            'attn_q_norm_g': attn_q_norm_g[l], 'attn_k_norm_g': attn_k_norm_g[l],
            'rwkv': (prep_w, post_w), 'w_out': w_out_b[l], 'ln_ffn_g': ln_ffn_g[l],
            'ffn_w_gate': wg_b[l], 'ffn_w_up': wu_b[l], 'ffn_conv_w': ffn_conv_w[l].astype(F32),
            'ffn_w_down': wd_b[l],
        })

    f = ffn_w_gate.shape[2]
    nh_a, nh_d = w // A_HD, w // D_HD
    zero_state = (None, None, jnp.zeros((bp, nh_a, A_HD, A_HD), F32), jnp.zeros((bp, w), F32),
                  jnp.zeros((bp, 3, w), F32), jnp.zeros((bp, nh_d, D_HD, D_HD), F32),
                  jnp.zeros((bp, n_in - 9 * w), F32), jnp.zeros((bp, 2, f), F32))
    tab_p = _rope_tables(jnp.arange(tp, dtype=jnp.int32))
    tab_s = _rope_tables(PAST_LEN + jnp.arange(PAD_T, dtype=jnp.int32))
    xp = x_prompt.astype(F32)
    xs = jnp.pad(x_sample.astype(F32), ((0, 0), (0, PAD_T - ts), (0, 0)))
    p_states, s_states = [], []
    for l in range(depth):
        xp, st_p = _layer(xp, tp, tab_p, zero_state, lb_all[l], layers[l], dims)
        p_states.append(st_p)
        s_in = (cache_attn_k[l].astype(F32), cache_attn_v[l].astype(F32), state_hgrn[l].astype(F32),
                state_lru_h[l].astype(F32), state_lru_conv[l].astype(F32), state_rwkv[l].astype(F32),
                state_rwkv_shift[l].astype(F32), state_ffn_conv[l].astype(F32))
        xs, st_s = _layer(xs, ts, tab_s, s_in, lb_all[l], layers[l], dims)
        s_states.append(st_s)

    dp, ds = x_prompt.dtype, x_sample.dtype
    nh_c = w // C_HD

    def stack(states, i, dt, shape=None):
        out = jnp.stack([st[i] for st in states]).astype(dt)
        return out if shape is None else out.reshape((depth,) + shape)

    outs = [xp.astype(dp), xs[:, :ts].astype(ds)]
    for states, bb, tt, dt in ((p_states, bp, min(C_WIN, tp), dp), (s_states, bs, ts, ds)):
        outs += [stack(states, 0, dt, (bb, tt, nh_c, C_HD)), stack(states, 1, dt, (bb, tt, nh_c, C_HD)),
                 stack(states, 2, dt), stack(states, 3, dt), stack(states, 4, dt), stack(states, 5, dt),
                 stack(states, 6, dt), stack(states, 7, dt)]
    return tuple(outs)
```

```python
import functools
import math

import numpy as np
import jax
import jax.numpy as jnp
from jax import lax
from jax.experimental import pallas as pl
from jax.experimental.pallas import tpu as pltpu

F32 = jnp.float32
BF16 = jnp.bfloat16

PAST_LEN = 8192
A_HD = 128
B_HD = 128
C_HD = 128
D_HD = 64
ROPE_DIM = C_HD // 4
ROPE_THETA = 500000.0
C_PATTERNS = ((128, 1), (512, 4), (2048, 16))
LRU_C = 8.0
RMS_EPS = 1e-6
RWKV_GN_EPS = 64e-5
NEG_BIG = -30000.0
F_FLOOR = 1e-20

LANE = 128
SUBLANE = 8
VMEM_LIMIT_BYTES = 56 * 2**20

C_WIN = max(win for win, _ in C_PATTERNS)

TIME_BLOCK_CAP = 256
MM_TM_CAP = 1024
FFN_TN = 256
SCAN_STEPS_CAP = 32

HGRN_CHUNK = 16
HGRN_ROW_GROUP = 64
PAD_T = 16
ATT_BLK = 128
MASK_BIAS = -1e30
HI = lax.Precision.HIGHEST


def _cp(*sem):
    return pltpu.CompilerParams(dimension_semantics=sem, vmem_limit_bytes=VMEM_LIMIT_BYTES)


def _pick(n, cap, quantum):
    if n <= cap:
        return n
    t = (cap // quantum) * quantum
    while t >= quantum:
        if n % t == 0:
            return t
        t -= quantum
    raise ValueError(f"no tile for {n} (cap {cap}, quantum {quantum})")


def _sigmoid(x):
    return jax.nn.sigmoid(x)


def _silu(x):
    return x * jax.nn.sigmoid(x)


def _softplus(x):
    return jnp.maximum(x, 0.0) + jnp.log(1.0 + jnp.exp(-jnp.abs(x)))


def _gelu_tanh(x):
    return x * (0.5 * (1.0 + jnp.tanh(math.sqrt(2.0 / math.pi) * (x + 0.044715 * (x * x * x)))))


def _norm_kernel(x_ref, g_ref, o_ref):
    x = x_ref[...]
    inv = lax.rsqrt(jnp.mean(x * x, axis=-1, keepdims=True) + RMS_EPS)
    o_ref[...] = (x * inv * g_ref[...]).astype(o_ref.dtype)


def _rmsnorm_bf16(x2, g):
    m, d = x2.shape
    tm = _pick(m, 256, SUBLANE)
    return pl.pallas_call(
        _norm_kernel,
        out_shape=jax.ShapeDtypeStruct((m, d), BF16),
        grid=(m // tm,),
        in_specs=[pl.BlockSpec((tm, d), lambda i: (i, 0)), pl.BlockSpec((1, d), lambda i: (0, 0))],
        out_specs=pl.BlockSpec((tm, d), lambda i: (i, 0)),
        compiler_params=_cp("parallel"),
        name="rmsnorm",
    )(x2, g.reshape(1, d).astype(F32))


def _mm_kernel(*refs, nk, has_res):
    if has_res:
        a_ref, w_ref, r_ref, o_ref = refs[:4]
    else:
        a_ref, w_ref, o_ref = refs[:3]
        r_ref = None
    part = jnp.dot(a_ref[...], w_ref[...], preferred_element_type=F32)
    if nk == 1:
        o_ref[...] = (r_ref[...] + part) if has_res else part
        return
    acc_ref = refs[-1]
    k = pl.program_id(2)

    @pl.when(k == 0)
    def _():
        acc_ref[...] = part

    @pl.when(k > 0)
    def _():
        acc_ref[...] += part

    @pl.when(k == nk - 1)
    def _():
        o_ref[...] = (r_ref[...] + acc_ref[...]) if has_res else acc_ref[...]


def _matmul(a, w, res=None, *, tm_cap, tn_cap, tk_cap, w_resident):
    m, k = a.shape
    n = w.shape[1]
    tm = _pick(m, tm_cap, SUBLANE)
    tn = _pick(n, tn_cap, LANE)
    tk = _pick(k, tk_cap, LANE)
    nk = k // tk
    if w_resident:
        grid = (n // tn, m // tm, nk)
        amap = lambda j, i, kk: (i, kk)
        wmap = lambda j, i, kk: (kk, j)
        omap = lambda j, i, kk: (i, j)
    else:
        grid = (m // tm, n // tn, nk)
        amap = lambda i, j, kk: (i, kk)
        wmap = lambda i, j, kk: (kk, j)
        omap = lambda i, j, kk: (i, j)
    in_specs = [pl.BlockSpec((tm, tk), amap), pl.BlockSpec((tk, tn), wmap)]
    args = [a, w]
    if res is not None:
        in_specs.append(pl.BlockSpec((tm, tn), omap))
        args.append(res)
    scratch = [pltpu.VMEM((tm, tn), F32)] if nk > 1 else []
    return pl.pallas_call(
        functools.partial(_mm_kernel, nk=nk, has_res=res is not None),
        out_shape=jax.ShapeDtypeStruct((m, n), F32),
        grid=grid,
        in_specs=in_specs,
        out_specs=pl.BlockSpec((tm, tn), omap),
        scratch_shapes=scratch,
        compiler_params=_cp("parallel", "parallel", "arbitrary"),
        name="matmul_res" if res is not None else "matmul",
    )(*args)


def _ffn_kernel(h_ref, wg_ref, wu_ref, cw_ref, st_ref, z_ref, so_ref, gs_ref, *, nseq, ts, i_state, r_state):
    i = pl.program_id(2)
    h = h_ref[...]
    gate = jnp.dot(h, wg_ref[...], preferred_element_type=F32)
    up = jnp.dot(h, wu_ref[...], preferred_element_type=F32)
    cw = cw_ref[...]
    for s in range(nseq):
        rows = slice(s * ts, (s + 1) * ts)

        @pl.when(i == 0)
        def _():
            gs_ref[s, 6:8, :] = st_ref[s]

        @pl.when(i > 0)
        def _():
            gs_ref[s, 0:8, :] = gs_ref[s, ts:ts + 8, :]

        g = gate[rows, :]
        gs_ref[s, 8:ts + 8, :] = g
        y = gs_ref[s, 6:ts + 6, :] * cw[0:1, :]
        y = y + gs_ref[s, 7:ts + 7, :] * cw[1:2, :]
        y = y + g * cw[2:3, :]
        z_ref[rows, :] = (_silu(y) * up[rows, :]).astype(z_ref.dtype)

        @pl.when(i == i_state)
        def _():
            so_ref[s] = gs_ref[s, 8 + r_state:10 + r_state, :]


def _ffn_gate_up(h3, wg, wu, cw, st, n_valid):
    b, t, d = h3.shape
    f = wg.shape[1]
    tn = _pick(f, FFN_TN, LANE)
    if b * t <= MM_TM_CAP:
        nseq, ts, nb, nt = b, t, 1, 1
    else:
        nseq, ts, nb, nt = 1, _pick(t, MM_TM_CAP, SUBLANE), b, t // _pick(t, MM_TM_CAP, SUBLANE)
    tm = nseq * ts
    i_state, r_state = divmod(n_valid - 2, ts)
    assert r_state + 2 <= ts
    z, so = pl.pallas_call(
        functools.partial(_ffn_kernel, nseq=nseq, ts=ts, i_state=i_state, r_state=r_state),
        out_shape=(jax.ShapeDtypeStruct((nb, nt * tm, f), BF16), jax.ShapeDtypeStruct((b, 2, f), F32)),
        grid=(f // tn, nb, nt),
        in_specs=[
            pl.BlockSpec((None, tm, d), lambda j, bb, i: (bb, i, 0)),
            pl.BlockSpec((d, tn), lambda j, bb, i: (0, j)),
            pl.BlockSpec((d, tn), lambda j, bb, i: (0, j)),
            pl.BlockSpec((3, tn), lambda j, bb, i: (0, j)),
            pl.BlockSpec((nseq, 2, tn), lambda j, bb, i: (bb, 0, j)),
        ],
        out_specs=(
            pl.BlockSpec((None, tm, tn), lambda j, bb, i: (bb, i, j)),
            pl.BlockSpec((nseq, 2, tn), lambda j, bb, i: (bb, 0, j)),
        ),
        scratch_shapes=[pltpu.VMEM((nseq, ts + 8, tn), F32)],
        compiler_params=_cp("parallel", "parallel", "arbitrary"),
        name="ffn_gate_up",
    )(h3.reshape(nb, nt * tm, d), wg, wu, cw, st)
    return z.reshape(b, t, f), so


def _hgrn_kernel(q_ref, f_ref, i_ref, g_ref, lb_ref, ng_ref, tri_ref, s0_ref, y_ref, so_ref,
                 st_ref, kp_ref, gp_ref, vp_ref, o_ref, qs_ref, ud_ref, *, tb, n_valid, masked, n_chunks):
    it = pl.program_id(2)
    ch = HGRN_CHUNK

    @pl.when(it == 0)
    def _():
        st_ref[...] = s0_ref[...]

    lb = lb_ref[...]
    fl = f_ref[...]
    f = lb + (1.0 - lb) * _sigmoid(fl)
    lf = jnp.log(jnp.maximum(f, F_FLOOR))
    kk = (1.0 - lb) * _sigmoid(-fl)
    row = lax.broadcasted_iota(jnp.int32, (tb, 1), 0)
    if masked:
        live = (it * tb + row) < n_valid
        lf = jnp.where(live, lf, 0.0)
        kk = jnp.where(live, kk, 0.0)
    q = _silu(q_ref[...]) * (A_HD ** -0.5)
    v = i_ref[...]
    g = jnp.dot(tri_ref[...], lf, precision=HI, preferred_element_type=F32)
    zero = jnp.zeros((ch, A_HD), F32)
    for ref, val in ((kp_ref, kk), (gp_ref, g), (vp_ref, v)):
        ref[0:ch, :] = zero
        ref[ch:tb + ch, :] = val
    qs_ref[...] = q
    qe = (q * jnp.exp(g)).astype(BF16)

    for c in range(n_chunks):
        lo = c * ch
        gl = gp_ref[lo + 2 * ch - 1:lo + 2 * ch, :]
        kt = kk[lo:lo + ch, :] * jnp.exp(gl - g[lo:lo + ch, :])
        ud_ref[c] = lax.dot_general(v[lo:lo + ch, :].astype(BF16), kt.astype(BF16),
                                    (((0,), (0,)), ((), ())), preferred_element_type=F32)
    st = st_ref[...]
    for c in range(n_chunks):
        lo = c * ch
        o_ref[lo:lo + ch, :] = lax.dot_general(qe[lo:lo + ch, :], st.astype(BF16),
                                               (((1,), (1,)), ((), ())), preferred_element_type=F32)
        st = st * jnp.exp(gp_ref[lo + 2 * ch - 1:lo + 2 * ch, :]) + ud_ref[c]
    st_ref[...] = st
    so_ref[...] = st

    grp = min(tb, HGRN_ROW_GROUP)
    pos = lax.broadcasted_iota(jnp.int32, (grp, 1), 0) & (ch - 1)
    for r0 in range(0, tb, grp):
        qg = qs_ref[r0:r0 + grp, :]
        gg = gp_ref[ch + r0:ch + r0 + grp, :]
        o = o_ref[r0:r0 + grp, :]
        for d in range(ch):
            lo = ch + r0 - d
            dec = jnp.exp(jnp.where(pos >= d, gg - gp_ref[lo:lo + grp, :], NEG_BIG))
            a = jnp.sum(qg * kp_ref[lo:lo + grp, :] * dec, axis=-1, keepdims=True)
            o = o + a * vp_ref[lo:lo + grp, :]
        o = o * lax.rsqrt(jnp.mean(o * o, axis=-1, keepdims=True) + RMS_EPS) * ng_ref[...]
        y_ref[r0:r0 + grp, :] = (o * _silu(g_ref[r0:r0 + grp, :])).astype(y_ref.dtype)


def _hgrn(proj3, lb, norm_g, s0_t, w, n_valid):
    b, t, _ = proj3.shape
    nh = w // A_HD
    tb = _pick(t, TIME_BLOCK_CAP, HGRN_CHUNK)
    ch = HGRN_CHUNK
    assert ch & (ch - 1) == 0
    idx = np.arange(tb)
    tri = ((idx[:, None] // ch == idx[None, :] // ch) & (idx[None, :] <= idx[:, None])).astype(np.float32)

    def col(off):
        return lambda bb, h, i: (bb, i, off * nh + h)

    return pl.pallas_call(
        functools.partial(_hgrn_kernel, tb=tb, n_valid=n_valid, masked=n_valid < t, n_chunks=tb // ch),
        out_shape=(jax.ShapeDtypeStruct((b, t, w), BF16), jax.ShapeDtypeStruct((b, nh, A_HD, A_HD), F32)),
        grid=(b, nh, t // tb),
        in_specs=[
            pl.BlockSpec((None, tb, A_HD), col(0)),
            pl.BlockSpec((None, tb, A_HD), col(1)),
            pl.BlockSpec((None, tb, A_HD), col(2)),
            pl.BlockSpec((None, tb, A_HD), col(3)),
            pl.BlockSpec((1, A_HD), lambda bb, h, i: (0, h)),
            pl.BlockSpec((1, A_HD), lambda bb, h, i: (0, 0)),
            pl.BlockSpec((tb, tb), lambda bb, h, i: (0, 0)),
            pl.BlockSpec((None, None, A_HD, A_HD), lambda bb, h, i: (bb, h, 0, 0)),
        ],
        out_specs=(
            pl.BlockSpec((None, tb, A_HD), lambda bb, h, i: (bb, i, h)),
            pl.BlockSpec((None, None, A_HD, A_HD), lambda bb, h, i: (bb, h, 0, 0)),
        ),
        scratch_shapes=[pltpu.VMEM((A_HD, A_HD), F32)] + [pltpu.VMEM((tb + ch, A_HD), F32)] * 3
        + [pltpu.VMEM((tb, A_HD), F32)] * 2 + [pltpu.VMEM((tb // ch, A_HD, A_HD), F32)],
        compiler_params=_cp("parallel", "parallel", "arbitrary"),
        name="hgrn2",
    )(proj3, proj3, proj3, proj3, lb.reshape(1, w), norm_g.reshape(1, A_HD), jnp.asarray(tri), s0_t)


def _lru_kernel(x_ref, g_ref, buf_ref, h0_ref, cw_ref, cb_ref, wa_ref, ba_ref, wx_ref, bx_ref, lam_ref,
                y_ref, hl_ref, bo_ref, xs_ref, a_ref, b_ref, hc_ref, *, tb, n_valid, masked, i_state, r_state):
    it = pl.program_id(1)
    nh = wa_ref.shape[0]

    @pl.when(it == 0)
    def _():
        xs_ref[5:8, :] = buf_ref[...]
        hc_ref[...] = h0_ref[...]

    @pl.when(it > 0)
    def _():
        xs_ref[0:8, :] = xs_ref[tb:tb + 8, :]

    x = x_ref[...]
    xs_ref[8:tb + 8, :] = x
    cw = cw_ref[...]
    xc = xs_ref[5:tb + 5, :] * cw[0:1, :]
    xc = xc + xs_ref[6:tb + 6, :] * cw[1:2, :]
    xc = xc + xs_ref[7:tb + 7, :] * cw[2:3, :]
    xc = xc + x * cw[3:4, :]
    xc = xc + cb_ref[...]
    xcb = xc.astype(BF16)
    ra, rx = [], []
    for h in range(nh):
        xh = xcb[:, h * B_HD:(h + 1) * B_HD]
        ra.append(jnp.dot(xh, wa_ref[h], preferred_element_type=F32))
        rx.append(jnp.dot(xh, wx_ref[h], preferred_element_type=F32))
    r = _sigmoid(jnp.concatenate(ra, axis=1) + ba_ref[...])
    ig = _sigmoid(jnp.concatenate(rx, axis=1) + bx_ref[...])
    log_a = (-LRU_C) * r * _softplus(-lam_ref[...])
    a = jnp.exp(log_a)
    om = -jnp.tanh(log_a) * (a * a + 1.0)
    bb = jnp.sqrt(jnp.maximum(om, 0.0)) * (ig * xc)
    if masked:
        row = it * tb + lax.broadcasted_iota(jnp.int32, (tb, 1), 0)
        live = row < n_valid
        a = jnp.where(live, a, 1.0)
        bb = jnp.where(live, bb, 0.0)
    a_ref[...] = a
    b_ref[...] = bb

    def step(t, h):
        h = a_ref[pl.ds(t, 1), :] * h + b_ref[pl.ds(t, 1), :]
        b_ref[pl.ds(t, 1), :] = h
        return h

    h = lax.fori_loop(0, tb, step, hc_ref[...], unroll=8)
    hc_ref[...] = h
    hl_ref[...] = h
    y_ref[...] = (b_ref[...] * _gelu_tanh(g_ref[...])).astype(y_ref.dtype)

    @pl.when(it == i_state)
    def _():
        bo_ref[...] = xs_ref[8 + r_state:11 + r_state, :]


def _lru(proj3, buf, h0, p, w, n_valid):
    b, t, _ = proj3.shape
    nh = w // B_HD
    tb = _pick(t, TIME_BLOCK_CAP, SUBLANE)
    i_state, r_state = divmod(n_valid - 3, tb)
    assert n_valid >= 3 and r_state + 3 <= tb
    row = lambda v: v.reshape(1, w).astype(F32)
    full = lambda shape: pl.BlockSpec(shape, lambda bb, i: (0,) * len(shape))
    return pl.pallas_call(
        functools.partial(_lru_kernel, tb=tb, n_valid=n_valid, masked=n_valid < t,
                          i_state=i_state, r_state=r_state),
        out_shape=(jax.ShapeDtypeStruct((b, t, w), BF16), jax.ShapeDtypeStruct((b, 1, w), F32),
                   jax.ShapeDtypeStruct((b, 3, w), F32)),
        grid=(b, t // tb),
        in_specs=[
            pl.BlockSpec((None, tb, w), lambda bb, i: (bb, i, 4)),
            pl.BlockSpec((None, tb, w), lambda bb, i: (bb, i, 5)),
            pl.BlockSpec((None, 3, w), lambda bb, i: (bb, 0, 0)),
            pl.BlockSpec((None, 1, w), lambda bb, i: (bb, 0, 0)),
            full((4, w)), full((1, w)), full((nh, B_HD, B_HD)), full((1, w)),
            full((nh, B_HD, B_HD)), full((1, w)), full((1, w)),
        ],
        out_specs=(
            pl.BlockSpec((None, tb, w), lambda bb, i: (bb, i, 0)),
            pl.BlockSpec((None, 1, w), lambda bb, i: (bb, 0, 0)),
            pl.BlockSpec((None, 3, w), lambda bb, i: (bb, 0, 0)),
        ),
        scratch_shapes=[pltpu.VMEM((tb + 8, w), F32), pltpu.VMEM((tb, w), F32), pltpu.VMEM((tb, w), F32),
                        pltpu.VMEM((1, w), F32)],
        compiler_params=_cp("parallel", "arbitrary"),
        name="rglru",
    )(proj3, proj3, buf, h0.reshape(b, 1, w), p['lru_conv_w'].astype(F32), row(p['lru_conv_b']),
      p['lru_wa'].astype(BF16), row(p['lru_ba']), p['lru_wx'].astype(BF16), row(p['lru_bx']),
      row(p['lru_lambda']))


def _qk_kernel(q_ref, k_ref, v_ref, qg_ref, kg_ref, c_ref, sa_ref, sb_ref, qo_ref, ko_ref, kb_ref, vb_ref):
    nh = q_ref.shape[1] // C_HD
    cosf, sina, sinb = c_ref[...], sa_ref[...], sb_ref[...]

    def prep(x, g):
        xn = x * lax.rsqrt(jnp.mean(x * x, axis=-1, keepdims=True) + RMS_EPS) * g
        up = pltpu.roll(xn, C_HD - ROPE_DIM // 2, axis=1)
        dn = pltpu.roll(xn, ROPE_DIM // 2, axis=1)
        return xn * cosf + up * sina + dn * sinb

    for h in range(nh):
        sl = slice(h * C_HD, (h + 1) * C_HD)
        qo_ref[:, sl] = (prep(q_ref[:, sl], qg_ref[...]) * (C_HD ** -0.5)).astype(qo_ref.dtype)
        kn = prep(k_ref[:, sl], kg_ref[...])
        ko_ref[:, sl] = kn
        kb_ref[:, sl] = kn.astype(kb_ref.dtype)
    vb_ref[...] = v_ref[...].astype(vb_ref.dtype)


def _rope_tables(pos):
    half = ROPE_DIM // 2
    inv = ROPE_THETA ** (-jnp.arange(half, dtype=F32) / half)
    ang = pos.astype(F32)[:, None] * inv[None, :]
    cos, sin = jnp.cos(ang), jnp.sin(ang)
    n = pos.shape[0]
    rest = C_HD - ROPE_DIM
    cosf = jnp.concatenate([cos, cos, jnp.ones((n, rest), F32)], axis=1)
    sina = jnp.concatenate([-sin, jnp.zeros((n, half + rest), F32)], axis=1)
    sinb = jnp.concatenate([jnp.zeros((n, half), F32), sin, jnp.zeros((n, rest), F32)], axis=1)
    return cosf, sina, sinb


def _qk_prep(proj3, qg, kg, tables, w):
    b, t, _ = proj3.shape
    tb = _pick(t, TIME_BLOCK_CAP, SUBLANE)
    tab = pl.BlockSpec((tb, C_HD), lambda bb, i: (i, 0))
    gsp = pl.BlockSpec((1, C_HD), lambda bb, i: (0, 0))
    src = lambda c: pl.BlockSpec((None, tb, w), lambda bb, i: (bb, i, c))
    dst = pl.BlockSpec((None, tb, w), lambda bb, i: (bb, i, 0))
    return pl.pallas_call(
        _qk_kernel,
        out_shape=(jax.ShapeDtypeStruct((b, t, w), BF16), jax.ShapeDtypeStruct((b, t, w), F32),
                   jax.ShapeDtypeStruct((b, t, w), BF16), jax.ShapeDtypeStruct((b, t, w), BF16)),
        grid=(b, t // tb),
        in_specs=[src(6), src(7), src(8), gsp, gsp, tab, tab, tab],
        out_specs=(dst, dst, dst, dst),
        compiler_params=_cp("parallel", "parallel"),
        name="qk_norm_rope",
    )(proj3, proj3, proj3, qg.reshape(1, C_HD).astype(F32), kg.reshape(1, C_HD).astype(F32), *tables)


def _pattern_bias(dist):
    dist = jnp.asarray(dist, jnp.int32)
    cnt = jnp.zeros(dist.shape, jnp.int32)
    for win, dil in C_PATTERNS:
        cnt = cnt + ((dist >= 0) & (dist <= win) & (dist % dil == 0)).astype(jnp.int32)
    return jnp.where(cnt > 0, jnp.log(jnp.maximum(cnt, 1).astype(F32)), MASK_BIAS).astype(F32)


def _attn_kernel(q_ref, k_ref, v_ref, bias_ref, o_ref, *, n_back, span):
    qi = pl.program_id(2)
    rows = pl.ds(pl.multiple_of(jnp.maximum(qi - n_back, 0) * ATT_BLK, ATT_BLK), span)
    s = lax.dot_general(q_ref[...], k_ref[rows, :], (((1,), (1,)), ((), ())), preferred_element_type=F32)
    s = s + bias_ref[...]
    m = jnp.max(s, axis=-1, keepdims=True)
    p = jnp.exp(s - m)
    l = jnp.sum(p, axis=-1, keepdims=True)
    acc = jnp.dot(p.astype(BF16), v_ref[rows, :], preferred_element_type=F32)
    o_ref[...] = (acc / l).astype(o_ref.dtype)


def _attn_prompt(qn, kb, vb, w):
    b, t, _ = qn.shape
    nh = w // C_HD
    blk = ATT_BLK
    assert t % blk == 0
    n_back = -(-C_WIN // blk)
    n_span = min(n_back + 1, t // blk)
    r = np.arange(blk)
    c = np.arange(n_span * blk)
    n_tab = min(n_back, n_span - 1) + 1
    bias = _pattern_bias(np.arange(n_tab)[:, None, None] * blk + r[None, :, None] - c[None, None, :])
    return pl.pallas_call(
        functools.partial(_attn_kernel, n_back=n_back, span=n_span * blk),
        out_shape=jax.ShapeDtypeStruct((b, t, w), BF16),
        grid=(b, nh, t // blk),
        in_specs=[
            pl.BlockSpec((None, blk, C_HD), lambda bb, h, i: (bb, i, h)),
            pl.BlockSpec((None, t, C_HD), lambda bb, h, i: (bb, 0, h)),
            pl.BlockSpec((None, t, C_HD), lambda bb, h, i: (bb, 0, h)),
            pl.BlockSpec((None, blk, n_span * blk), lambda bb, h, i: (jnp.minimum(i, n_tab - 1), 0, 0)),
        ],
        out_specs=pl.BlockSpec((None, blk, C_HD), lambda bb, h, i: (bb, i, h)),
        compiler_params=_cp("parallel", "parallel", "arbitrary"),
        name="attn_prompt",
    )(qn, kb, vb, bias)


def _attn_s_kernel(q_ref, kp_ref, vp_ref, kn_ref, vn_ref, bp_ref, bn_ref, o_ref):
    q = q_ref[...]
    nt = (((1,), (1,)), ((), ()))
    sp = lax.dot_general(q, kp_ref[...].astype(BF16), nt, preferred_element_type=F32) + bp_ref[...]
    sn = lax.dot_general(q, kn_ref[...], nt, preferred_element_type=F32) + bn_ref[...]
    m = jnp.maximum(jnp.max(sp, axis=-1, keepdims=True), jnp.max(sn, axis=-1, keepdims=True))
    pp = jnp.exp(sp - m)
    pn = jnp.exp(sn - m)
    l = jnp.sum(pp, axis=-1, keepdims=True) + jnp.sum(pn, axis=-1, keepdims=True)
    acc = jnp.dot(pp.astype(BF16), vp_ref[...].astype(BF16), preferred_element_type=F32)
    acc = acc + jnp.dot(pn.astype(BF16), vn_ref[...], preferred_element_type=F32)
    o_ref[...] = (acc / l).astype(o_ref.dtype)


def _attn_sample(qn, kpast, vpast, kb, vb, w, n_valid):
    b, tp, _ = qn.shape
    lp = kpast.shape[1]
    nh = w // C_HD
    t = np.arange(tp)
    live = t < n_valid
    bp = jnp.where(live[:, None], _pattern_bias(lp + t[:, None] - np.arange(lp)[None, :]), 0.0)
    bn = jnp.where(live[:, None] & live[None, :], _pattern_bias(t[:, None] - t[None, :]), MASK_BIAS)
    new = pl.BlockSpec((None, tp, C_HD), lambda bb, h: (bb, 0, h))
    past = pl.BlockSpec((None, lp, C_HD), lambda bb, h: (bb, 0, h))
    return pl.pallas_call(
        _attn_s_kernel,
        out_shape=jax.ShapeDtypeStruct((b, tp, w), BF16),
        grid=(b, nh),
        in_specs=[new, past, past, new, new,
                  pl.BlockSpec((tp, lp), lambda bb, h: (0, 0)),
                  pl.BlockSpec((tp, tp), lambda bb, h: (0, 0))],
        out_specs=new,
        compiler_params=_cp("parallel", "parallel"),
        name="attn_sample",
    )(qn, kpast, vpast, kb, vb, bp, bn)


def _seg_sum(x, e_ref):
    nb = x.shape[1] // LANE
    e = e_ref[...]
    return jnp.concatenate(
        [jnp.dot(x[:, i * LANE:(i + 1) * LANE], e, precision=HI, preferred_element_type=F32) for i in range(nb)],
        axis=1)


def _rwkv_prep_kernel(r_ref, k_ref, v_ref, x_ref, sr_ref, sk_ref, sv_ref, sx_ref,
                      mr_ref, mk_ref, mv_ref, mx_ref, w0_ref, w2_ref, a0_ref, a2_ref, g2_ref,
                      kk_ref, ka_ref, e_ref,
                      ro_ref, wo_ref, ko_ref, vo_ref, zo_ref, bo_ref, go_ref,
                      pr_ref, pk_ref, pv_ref, px_ref, *, tb, n_valid, masked):
    it = pl.program_id(1)

    @pl.when(it == 0)
    def _():
        pr_ref[...] = sr_ref[...]
        pk_ref[...] = sk_ref[...]
        pv_ref[...] = sv_ref[...]
        px_ref[...] = sx_ref[...]

    first = lax.broadcasted_iota(jnp.int32, (tb, 1), 0) == 0

    def shifted(x_ref_, prev_ref, mu_ref):
        x = x_ref_[...]
        prev = jnp.where(first, prev_ref[...], pltpu.roll(x, 1, axis=0))
        prev_ref[...] = x[tb - 1:tb, :]
        return x + mu_ref[...] * (prev - x)

    r = shifted(r_ref, pr_ref, mr_ref)
    k = shifted(k_ref, pk_ref, mk_ref)
    v = shifted(v_ref, pv_ref, mv_ref)
    x = shifted(x_ref, px_ref, mx_ref)
    lo = x[:, 0:LANE]
    hi = x[:, LANE:]
    mw = jnp.dot(jnp.tanh(lo).astype(BF16), w2_ref[...], preferred_element_type=F32)
    ma = jnp.dot(lo.astype(BF16), a2_ref[...], preferred_element_type=F32)
    g = jnp.dot(_sigmoid(hi).astype(BF16), g2_ref[...], preferred_element_type=F32)
    log_w = -_softplus(-(w0_ref[...] + mw)) - 0.5
    decay = jnp.exp(-jnp.exp(log_w))
    a = _sigmoid(a0_ref[...] + ma)
    kk = k * kk_ref[...]
    kk = kk / jnp.maximum(jnp.sqrt(_seg_sum(kk * kk, e_ref)), 1e-12)
    k2 = k * (1.0 + (a - 1.0) * ka_ref[...])
    bvec = kk * a
    if masked:
        live = (it * tb + lax.broadcasted_iota(jnp.int32, (tb, 1), 0)) < n_valid
        decay = jnp.where(live, decay, 1.0)
        k2 = jnp.where(live, k2, 0.0)
        bvec = jnp.where(live, bvec, 0.0)
    ro_ref[...] = r
    wo_ref[...] = decay
    ko_ref[...] = k2
    vo_ref[...] = v
    zo_ref[...] = -kk
    bo_ref[...] = bvec
    go_ref[...] = g


def _rwkv_prep(proj3, shift, pw, w, xw, n_valid):
    b, t, _ = proj3.shape
    tb = _pick(t, TIME_BLOCK_CAP, SUBLANE)
    xcol = (12 * w) // xw
    big = lambda c: pl.BlockSpec((None, tb, w), lambda bb, i: (bb, i, c))
    st = lambda n: pl.BlockSpec((None, 1, n), lambda bb, i: (bb, 0, 0))
    full = lambda shape: pl.BlockSpec(shape, lambda bb, i: (0,) * len(shape))
    out = pl.BlockSpec((None, tb, w), lambda bb, i: (bb, i, 0))
    sr, sk, sv, sx = shift
    return pl.pallas_call(
        functools.partial(_rwkv_prep_kernel, tb=tb, n_valid=n_valid, masked=n_valid < t),
        out_shape=tuple(jax.ShapeDtypeStruct((b, t, w), F32) for _ in range(7)),
        grid=(b, t // tb),
        in_specs=[big(9), big(10), big(11), pl.BlockSpec((None, tb, xw), lambda bb, i: (bb, i, xcol)),
                  st(w), st(w), st(w), st(xw),
                  full((1, w)), full((1, w)), full((1, w)), full((1, xw)),
                  full((1, w)), full((LANE, w)), full((1, w)), full((LANE, w)), full((xw - LANE, w)),
                  full((1, w)), full((1, w)), full((LANE, LANE))],
        out_specs=tuple(out for _ in range(7)),
        scratch_shapes=[pltpu.VMEM((1, w), F32)] * 3 + [pltpu.VMEM((1, xw), F32)],
        compiler_params=_cp("parallel", "arbitrary"),
        name="rwkv_prep",
    )(proj3, proj3, proj3, proj3, sr, sk, sv, sx, *pw)


def _rwkv_scan_kernel(r_ref, w_ref, k_ref, z_ref, b_ref, v_ref, s0_ref, o_ref, so_ref, s_ref, *, tc):
    nk = s_ref.shape[0]

    @pl.when(pl.program_id(0) == 0)
    def _():
        s_ref[...] = s0_ref[...]

    def tree(parts):
        while len(parts) > 1:
            parts = [parts[i] + parts[i + 1] for i in range(0, len(parts), 2)]
        return parts[0]

    def step(t, carry):
        z = z_ref[t]
        acc = [None] * 4
        for kx in range(nk):
            term = s_ref[kx] * z[kx:kx + 1, :]
            acc[kx % 4] = term if acc[kx % 4] is None else acc[kx % 4] + term
        sz = tree(acc)
        wv, bv, kv, rv, vv = w_ref[t], b_ref[t], k_ref[t], r_ref[t], v_ref[t]
        acc = [None] * 4
        for kx in range(nk):
            sl = slice(kx, kx + 1)
            sn = s_ref[kx] * wv[sl, :] + sz * bv[sl, :] + vv * kv[sl, :]
            s_ref[kx] = sn
            term = sn * rv[sl, :]
            acc[kx % 4] = term if acc[kx % 4] is None else acc[kx % 4] + term
        o_ref[t] = tree(acc)
        return carry

    lax.fori_loop(0, tc, step, 0)
    so_ref[...] = s_ref[...]


def _rwkv_scan(rl, wl, kl, zl, bl, vl, s0l):
    t, nk, _ = rl.shape
    nv = vl.shape[1]
    tc = _pick(t, SCAN_STEPS_CAP, 1)
    tok = pl.BlockSpec((tc, nk, LANE), lambda i: (i, 0, 0))
    vsp = pl.BlockSpec((tc, nv, LANE), lambda i: (i, 0, 0))
    ssp = pl.BlockSpec((nk, nv, LANE), lambda i: (0, 0, 0))
    return pl.pallas_call(
        functools.partial(_rwkv_scan_kernel, tc=tc),
        out_shape=(jax.ShapeDtypeStruct((t, nv, LANE), F32), jax.ShapeDtypeStruct((nk, nv, LANE), F32)),
        grid=(t // tc,),
        in_specs=[tok, tok, tok, tok, tok, vsp, ssp],
        out_specs=(vsp, ssp),
        scratch_shapes=[pltpu.VMEM((nk, nv, LANE), F32)],
        compiler_params=_cp("arbitrary"),
        name="rwkv_scan",
    )(rl, wl, kl, zl, bl, vl, s0l)


def _rwkv_post_kernel(o_ref, r_ref, k_ref, v_ref, g_ref, lg_ref, lb_ref, rk_ref, e_ref, y_ref):
    o = o_ref[...]
    inv_n = 1.0 / D_HD
    mu = _seg_sum(o, e_ref) * inv_n
    d = o - mu
    var = _seg_sum(d * d, e_ref) * inv_n
    on = d * lax.rsqrt(var + RWKV_GN_EPS) * lg_ref[...] + lb_ref[...]
    bonus = _seg_sum(r_ref[...] * k_ref[...] * rk_ref[...], e_ref) * v_ref[...]
    y_ref[...] = ((on + bonus) * g_ref[...]).astype(y_ref.dtype)


def _rwkv_post(o3, r3, k3, v3, g3, lg, lb, rk, e):
    b, t, w = o3.shape
    tb = _pick(t, TIME_BLOCK_CAP, SUBLANE)
    big = pl.BlockSpec((None, tb, w), lambda bb, i: (bb, i, 0))
    rowsp = pl.BlockSpec((1, w), lambda bb, i: (0, 0))
    return pl.pallas_call(
        _rwkv_post_kernel,
        out_shape=jax.ShapeDtypeStruct((b, t, w), BF16),
        grid=(b, t // tb),
        in_specs=[big] * 5 + [rowsp] * 3 + [pl.BlockSpec((LANE, LANE), lambda bb, i: (0, 0))],
        out_specs=big,
        compiler_params=_cp("parallel", "parallel"),
        name="rwkv_post",
    )(o3, r3, k3, v3, g3, lg, lb, rk, e)


def _rwkv(proj3, shift, s0, pw, w, xw, n_valid):
    b, t, _ = proj3.shape
    nh = w // D_HD
    chains = b * nh
    assert LANE % chains == 0 and D_HD % (LANE // chains) == 0
    vh = LANE // chains
    nv = D_HD // vh
    prep_w, (lg, lb, rk, e) = pw
    r3, w3, k3, v3, z3, b3, g3 = _rwkv_prep(proj3, shift, prep_w, w, xw, n_valid)

    def klay(x3):
        x = jnp.broadcast_to(x3.reshape(1, b, t, nh, D_HD), (vh, b, t, nh, D_HD))
        return x.transpose(2, 4, 0, 1, 3).reshape(t, D_HD, LANE)

    def vlay(x3):
        x = x3.reshape(b, t, nh, vh, nv).transpose(1, 4, 3, 0, 2)
        return x.reshape(t, nv, LANE)

    s0l = s0.reshape(b, nh, vh, nv, D_HD).transpose(4, 3, 2, 0, 1).reshape(D_HD, nv, LANE)
    ol, sl = _rwkv_scan(klay(r3), klay(w3), klay(k3), klay(z3), klay(b3), vlay(v3), s0l)
    o3 = ol.reshape(t, nv, vh, b, nh).transpose(3, 0, 4, 2, 1).reshape(b, t, w)
    s_new = sl.reshape(D_HD, nv, vh, b, nh).transpose(3, 4, 2, 1, 0).reshape(b, nh, D_HD, D_HD)
    y = _rwkv_post(o3, r3, k3, v3, g3, lg, lb, rk, e)
    return y, s_new


def _layer(x3, n_valid, tables, state, lbv, p, dims):
    d, w, xw, n_in = dims
    b, t, _ = x3.shape
    k_past, v_past, s_a, h_b, buf_b, s_d, shift_d, buf_f = state
    m = b * t
    x2 = x3.reshape(m, d)
    h = _rmsnorm_bf16(x2, p['ln_mix_g'])
    proj = _matmul(h, p['w_in'], tm_cap=MM_TM_CAP, tn_cap=1152, tk_cap=d, w_resident=True)
    proj3 = proj.reshape(b, t, -1)
    y_a, s_a_t = _hgrn(proj3, lbv, p['hgrn_norm_g'].astype(F32), jnp.swapaxes(s_a, -1, -2), w, n_valid)
    s_a_new = jnp.swapaxes(s_a_t, -1, -2)
    y_b, h_b_new, buf_b_new = _lru(proj3, buf_b, h_b, p, w, n_valid)
    qn, kn, kb, vb = _qk_prep(proj3, p['attn_q_norm_g'], p['attn_k_norm_g'], tables, w)
    if k_past is None:
        y_c = _attn_prompt(qn, kb, vb, w)
    else:
        lp = k_past.shape[1]
        y_c = _attn_sample(qn, k_past.reshape(b, lp, w), v_past.reshape(b, lp, w), kb, vb, w, n_valid)
    v_new = proj3[:, :, 8 * w:9 * w]
    sh = shift_d.reshape(b, 1, -1)
    shift = (sh[:, :, 0:w], sh[:, :, w:2 * w], sh[:, :, 2 * w:3 * w],
             jnp.pad(sh[:, :, 3 * w:], ((0, 0), (0, 0), (0, xw - (sh.shape[2] - 3 * w)))))
    y_d, s_d_new = _rwkv(proj3, shift, s_d, p['rwkv'], w, xw, n_valid)
    shift_new = proj3[:, n_valid - 1, 9 * w:n_in]
    y_mix = jnp.concatenate([y_a, y_b, y_c, y_d], axis=-1).reshape(m, 4 * w)
    x2 = _matmul(y_mix, p['w_out'], x2, tm_cap=MM_TM_CAP, tn_cap=512, tk_cap=4 * w, w_resident=True)
    h = _rmsnorm_bf16(x2, p['ln_ffn_g'])
    z, buf_f_new = _ffn_gate_up(h.reshape(b, t, d), p['ffn_w_gate'], p['ffn_w_up'], p['ffn_conv_w'], buf_f, n_valid)
    f = z.shape[-1]
    x2 = _matmul(z.reshape(m, f), p['ffn_w_down'], x2, tm_cap=MM_TM_CAP, tn_cap=512, tk_cap=f // 2,
                 w_resident=False)
    keep = slice(max(n_valid - C_WIN, 0), n_valid) if k_past is None else slice(0, n_valid)
    new_state = (kn[:, keep], v_new[:, keep], s_a_new, h_b_new.reshape(b, w), buf_b_new,
                 s_d_new, shift_new, buf_f_new)
    return x2.reshape(b, t, d), new_state


def kernel(x_prompt, x_sample, cache_attn_k, cache_attn_v, state_hgrn, state_lru_h, state_lru_conv, state_rwkv, state_rwkv_shift, state_ffn_conv, ln_mix_g, w_in, hgrn_lb_logits, hgrn_norm_g, lru_conv_w, lru_conv_b, lru_wa, lru_ba, lru_wx, lru_bx, lru_lambda, attn_q_norm_g, attn_k_norm_g, rwkv_mu, rwkv_w0, rwkv_w2, rwkv_a0, rwkv_a2, rwkv_g2, rwkv_k_k, rwkv_k_a, rwkv_r_k, rwkv_ln_g, rwkv_ln_b, w_out, ln_ffn_g, ffn_w_gate, ffn_w_up, ffn_conv_w, ffn_w_down):
    bp, tp, d = x_prompt.shape
    bs, ts, _ = x_sample.shape
    depth = w_in.shape[0]
    n_in = w_in.shape[2]
    w = d // 4
    n_pad = -(-n_in // LANE) * LANE
    rank_w, rank_a, rank_g = rwkv_w2.shape[1], rwkv_a2.shape[1], rwkv_g2.shape[1]
    assert rank_w + rank_a == LANE
    xw = n_pad - 12 * w
    dims = (d, w, xw, n_in)
    row = lambda v: v.reshape(1, -1).astype(F32)

    lb_soft = jax.nn.softmax(hgrn_lb_logits.astype(F32), axis=0)
    lb_all = jnp.cumsum(lb_soft, axis=0) - lb_soft[:1]

    w_in_b = jnp.pad(w_in, ((0, 0), (0, 0), (0, n_pad - n_in))).astype(BF16)
    w_out_b = w_out.astype(BF16)
    wg_b, wu_b, wd_b = ffn_w_gate.astype(BF16), ffn_w_up.astype(BF16), ffn_w_down.astype(BF16)
    zeros = lambda r: jnp.zeros((r, w), F32)
    head_id = np.arange(LANE) // D_HD
    e_blk = jnp.asarray((head_id[:, None] == head_id[None, :]).astype(np.float32))

    layers = []
    for l in range(depth):
        mu = rwkv_mu[l].astype(F32)
        prep_w = (
            row(mu[0:w]), row(mu[w:2 * w]), row(mu[2 * w:3 * w]),
            row(jnp.pad(mu[3 * w:], (0, xw - (mu.shape[0] - 3 * w)))),
            row(rwkv_w0[l]),
            jnp.concatenate([rwkv_w2[l].astype(F32), zeros(rank_a)], axis=0).astype(BF16),
            row(rwkv_a0[l]),
            jnp.concatenate([zeros(rank_w), rwkv_a2[l].astype(F32)], axis=0).astype(BF16),
            jnp.concatenate([rwkv_g2[l].astype(F32), zeros(xw - LANE - rank_g)], axis=0).astype(BF16),
            row(rwkv_k_k[l]), row(rwkv_k_a[l]), e_blk,
        )
        post_w = (row(rwkv_ln_g[l]), row(rwkv_ln_b[l]), row(rwkv_r_k[l]), e_blk)
        layers.append({
            'ln_mix_g': ln_mix_g[l], 'w_in': w_in_b[l], 'hgrn_norm_g': hgrn_norm_g[l],
            'lru_conv_w': lru_conv_w[l], 'lru_conv_b': lru_conv_b[l], 'lru_wa': lru_wa[l],
            'lru_ba': lru_ba[l], 'lru_wx': lru_wx[l], 'lru_bx': lru_bx[l], 'lru_lambda': lru_lambda[l],
            'attn_q_norm_g': attn_q_norm_g[l], 'attn_k_norm_g': attn_k_norm_g[l],
            'rwkv': (prep_w, post_w), 'w_out': w_out_b[l], 'ln_ffn_g': ln_ffn_g[l],
            'ffn_w_gate': wg_b[l], 'ffn_w_up': wu_b[l], 'ffn_conv_w': ffn_conv_w[l].astype(F32),
            'ffn_w_down': wd_b[l],
        })

    f = ffn_w_gate.shape[2]
    nh_a, nh_d = w // A_HD, w // D_HD
    zero_state = (None, None, jnp.zeros((bp, nh_a, A_HD, A_HD), F32), jnp.zeros((bp, w), F32),
                  jnp.zeros((bp, 3, w), F32), jnp.zeros((bp, nh_d, D_HD, D_HD), F32),
                  jnp.zeros((bp, n_in - 9 * w), F32), jnp.zeros((bp, 2, f), F32))
    tab_p = _rope_tables(jnp.arange(tp, dtype=jnp.int32))
    tab_s = _rope_tables(PAST_LEN + jnp.arange(PAD_T, dtype=jnp.int32))
    xp = x_prompt.astype(F32)
    xs = jnp.pad(x_sample.astype(F32), ((0, 0), (0, PAD_T - ts), (0, 0)))
    p_states, s_states = [], []
    for l in range(depth):
        xp, st_p = _layer(xp, tp, tab_p, zero_state, lb_all[l], layers[l], dims)
        p_states.append(st_p)
        s_in = (cache_attn_k[l].astype(F32), cache_attn_v[l].astype(F32), state_hgrn[l].astype(F32),
                state_lru_h[l].astype(F32), state_lru_conv[l].astype(F32), state_rwkv[l].astype(F32),
                state_rwkv_shift[l].astype(F32), state_ffn_conv[l].astype(F32))
        xs, st_s = _layer(xs, ts, tab_s, s_in, lb_all[l], layers[l], dims)
        s_states.append(st_s)

    dp, ds = x_prompt.dtype, x_sample.dtype
    nh_c = w // C_HD

    def stack(states, i, dt, shape=None):
        out = jnp.stack([st[i] for st in states]).astype(dt)
        return out if shape is None else out.reshape((depth,) + shape)

    outs = [xp.astype(dp), xs[:, :ts].astype(ds)]
    for states, bb, tt, dt in ((p_states, bp, min(C_WIN, tp), dp), (s_states, bs, ts, ds)):
        outs += [stack(states, 0, dt, (bb, tt, nh_c, C_HD)), stack(states, 1, dt, (bb, tt, nh_c, C_HD)),
                 stack(states, 2, dt), stack(states, 3, dt), stack(states, 4, dt), stack(states, 5, dt),
                 stack(states, 6, dt), stack(states, 7, dt)]
    return tuple(outs)
```

```python
import functools
import math

import numpy as np
import jax
import jax.numpy as jnp
from jax import lax
from jax.experimental import pallas as pl
from jax.experimental.pallas import tpu as pltpu

F32 = jnp.float32
BF16 = jnp.bfloat16

PAST_LEN = 8192
A_HD = 128
B_HD = 128
C_HD = 128
D_HD = 64
ROPE_DIM = C_HD // 4
ROPE_THETA = 500000.0
C_PATTERNS = ((128, 1), (512, 4), (2048, 16))
LRU_C = 8.0
RMS_EPS = 1e-6
RWKV_GN_EPS = 64e-5
NEG_BIG = -30000.0
F_FLOOR = 1e-20

LANE = 128
SUBLANE = 8
VMEM_LIMIT_BYTES = 56 * 2**20

C_WIN = max(win for win, _ in C_PATTERNS)

TIME_BLOCK_CAP = 256
MM_TM_CAP = 1024
FFN_TN = 256
SCAN_STEPS_CAP = 32
SCAN_V_GROUP = 32
SCAN_K_CHUNK = 32

HGRN_CHUNK = 16
HGRN_ROW_GROUP = 64
PAD_T = 16
ATT_BLK = 256
MASK_BIAS = -1e30
HI = lax.Precision.HIGHEST


def _cp(*sem):
    return pltpu.CompilerParams(dimension_semantics=sem, vmem_limit_bytes=VMEM_LIMIT_BYTES)


def _pick(n, cap, quantum):
    if n <= cap:
        return n
    t = (cap // quantum) * quantum
    while t >= quantum:
        if n % t == 0:
            return t
        t -= quantum
    raise ValueError(f"no tile for {n} (cap {cap}, quantum {quantum})")


def _sigmoid(x):
    return jax.nn.sigmoid(x)


def _silu(x):
    return x * jax.nn.sigmoid(x)


def _softplus(x):
    return jnp.maximum(x, 0.0) + jnp.log(1.0 + jnp.exp(-jnp.abs(x)))


def _gelu_tanh(x):
    return x * (0.5 * (1.0 + jnp.tanh(math.sqrt(2.0 / math.pi) * (x + 0.044715 * (x * x * x)))))


def _norm_kernel(x_ref, g_ref, o_ref):
    x = x_ref[...]
    inv = lax.rsqrt(jnp.mean(x * x, axis=-1, keepdims=True) + RMS_EPS)
    o_ref[...] = (x * inv * g_ref[...]).astype(o_ref.dtype)


def _rmsnorm_bf16(x2, g):
    m, d = x2.shape
    tm = _pick(m, 256, SUBLANE)
    return pl.pallas_call(
        _norm_kernel,
        out_shape=jax.ShapeDtypeStruct((m, d), BF16),
        grid=(m // tm,),
        in_specs=[pl.BlockSpec((tm, d), lambda i: (i, 0)), pl.BlockSpec((1, d), lambda i: (0, 0))],
        out_specs=pl.BlockSpec((tm, d), lambda i: (i, 0)),
        compiler_params=_cp("parallel"),
        name="rmsnorm",
    )(x2, g.reshape(1, d).astype(F32))


def _mm_kernel(*refs, nk, has_res):
    if has_res:
        a_ref, w_ref, r_ref, o_ref = refs[:4]
    else:
        a_ref, w_ref, o_ref = refs[:3]
        r_ref = None
    part = jnp.dot(a_ref[...], w_ref[...], preferred_element_type=F32)
    if nk == 1:
        o_ref[...] = (r_ref[...] + part) if has_res else part
        return
    acc_ref = refs[-1]
    k = pl.program_id(2)

    @pl.when(k == 0)
    def _():
        acc_ref[...] = part

    @pl.when(k > 0)
    def _():
        acc_ref[...] += part

    @pl.when(k == nk - 1)
    def _():
        o_ref[...] = (r_ref[...] + acc_ref[...]) if has_res else acc_ref[...]


def _matmul(a, w, res=None, *, tm_cap, tn_cap, tk_cap, w_resident):
    m, k = a.shape
    n = w.shape[1]
    tm = _pick(m, tm_cap, SUBLANE)
    tn = _pick(n, tn_cap, LANE)
    tk = _pick(k, tk_cap, LANE)
    nk = k // tk
    if w_resident:
        grid = (n // tn, m // tm, nk)
        amap = lambda j, i, kk: (i, kk)
        wmap = lambda j, i, kk: (kk, j)
        omap = lambda j, i, kk: (i, j)
    else:
        grid = (m // tm, n // tn, nk)
        amap = lambda i, j, kk: (i, kk)
        wmap = lambda i, j, kk: (kk, j)
        omap = lambda i, j, kk: (i, j)
    in_specs = [pl.BlockSpec((tm, tk), amap), pl.BlockSpec((tk, tn), wmap)]
    args = [a, w]
    if res is not None:
        in_specs.append(pl.BlockSpec((tm, tn), omap))
        args.append(res)
    scratch = [pltpu.VMEM((tm, tn), F32)] if nk > 1 else []
    return pl.pallas_call(
        functools.partial(_mm_kernel, nk=nk, has_res=res is not None),
        out_shape=jax.ShapeDtypeStruct((m, n), F32),
        grid=grid,
        in_specs=in_specs,
        out_specs=pl.BlockSpec((tm, tn), omap),
        scratch_shapes=scratch,
        compiler_params=_cp("parallel", "parallel", "arbitrary"),
        name="matmul_res" if res is not None else "matmul",
    )(*args)


def _ffn_kernel(h_ref, wg_ref, wu_ref, cw_ref, st_ref, z_ref, so_ref, gs_ref, *, nseq, ts, i_state, r_state):
    i = pl.program_id(2)
    h = h_ref[...]
    gate = jnp.dot(h, wg_ref[...], preferred_element_type=F32)
    up = jnp.dot(h, wu_ref[...], preferred_element_type=F32)
    cw = cw_ref[...]
    for s in range(nseq):
        rows = slice(s * ts, (s + 1) * ts)

        @pl.when(i == 0)
        def _():
            gs_ref[s, 6:8, :] = st_ref[s]

        @pl.when(i > 0)
        def _():
            gs_ref[s, 0:8, :] = gs_ref[s, ts:ts + 8, :]

        g = gate[rows, :]
        gs_ref[s, 8:ts + 8, :] = g
        y = gs_ref[s, 6:ts + 6, :] * cw[0:1, :]
        y = y + gs_ref[s, 7:ts + 7, :] * cw[1:2, :]
        y = y + g * cw[2:3, :]
        z_ref[rows, :] = (_silu(y) * up[rows, :]).astype(z_ref.dtype)

        @pl.when(i == i_state)
        def _():
            so_ref[s] = gs_ref[s, 8 + r_state:10 + r_state, :]


def _ffn_gate_up(h3, wg, wu, cw, st, n_valid):
    b, t, d = h3.shape
    f = wg.shape[1]
    tn = _pick(f, FFN_TN, LANE)
    if b * t <= MM_TM_CAP:
        nseq, ts, nb, nt = b, t, 1, 1
    else:
        nseq, ts, nb, nt = 1, _pick(t, MM_TM_CAP, SUBLANE), b, t // _pick(t, MM_TM_CAP, SUBLANE)
    tm = nseq * ts
    i_state, r_state = divmod(n_valid - 2, ts)
    assert r_state + 2 <= ts
    z, so = pl.pallas_call(
        functools.partial(_ffn_kernel, nseq=nseq, ts=ts, i_state=i_state, r_state=r_state),
        out_shape=(jax.ShapeDtypeStruct((nb, nt * tm, f), BF16), jax.ShapeDtypeStruct((b, 2, f), F32)),
        grid=(f // tn, nb, nt),
        in_specs=[
            pl.BlockSpec((None, tm, d), lambda j, bb, i: (bb, i, 0)),
            pl.BlockSpec((d, tn), lambda j, bb, i: (0, j)),
            pl.BlockSpec((d, tn), lambda j, bb, i: (0, j)),
            pl.BlockSpec((3, tn), lambda j, bb, i: (0, j)),
            pl.BlockSpec((nseq, 2, tn), lambda j, bb, i: (bb, 0, j)),
        ],
        out_specs=(
            pl.BlockSpec((None, tm, tn), lambda j, bb, i: (bb, i, j)),
            pl.BlockSpec((nseq, 2, tn), lambda j, bb, i: (bb, 0, j)),
        ),
        scratch_shapes=[pltpu.VMEM((nseq, ts + 8, tn), F32)],
        compiler_params=_cp("parallel", "parallel", "arbitrary"),
        name="ffn_gate_up",
    )(h3.reshape(nb, nt * tm, d), wg, wu, cw, st)
    return z.reshape(b, t, f), so


def _hgrn_kernel(q_ref, f_ref, i_ref, g_ref, lb_ref, ng_ref, tri_ref, s0_ref, y_ref, so_ref,
                 st_ref, kp_ref, gp_ref, vp_ref, o_ref, qs_ref, ud_ref, *, tb, n_valid, masked, n_chunks):
    it = pl.program_id(2)
    ch = HGRN_CHUNK

    @pl.when(it == 0)
    def _():
        st_ref[...] = s0_ref[...]

    lb = lb_ref[...]
    fl = f_ref[...]
    f = lb + (1.0 - lb) * _sigmoid(fl)
    lf = jnp.log(jnp.maximum(f, F_FLOOR))
    kk = (1.0 - lb) * _sigmoid(-fl)
    row = lax.broadcasted_iota(jnp.int32, (tb, 1), 0)
    if masked:
        live = (it * tb + row) < n_valid
        lf = jnp.where(live, lf, 0.0)
        kk = jnp.where(live, kk, 0.0)
    q = _silu(q_ref[...]) * (A_HD ** -0.5)
    v = i_ref[...]
    g = jnp.dot(tri_ref[...], lf, precision=HI, preferred_element_type=F32)
    zero = jnp.zeros((ch, A_HD), F32)
    for ref, val in ((kp_ref, kk), (gp_ref, g), (vp_ref, v)):
        ref[0:ch, :] = zero
        ref[ch:tb + ch, :] = val
    qs_ref[...] = q
    qe = (q * jnp.exp(g)).astype(BF16)

    for c in range(n_chunks):
        lo = c * ch
        gl = gp_ref[lo + 2 * ch - 1:lo + 2 * ch, :]
        kt = kk[lo:lo + ch, :] * jnp.exp(gl - g[lo:lo + ch, :])
        ud_ref[c] = lax.dot_general(v[lo:lo + ch, :].astype(BF16), kt.astype(BF16),
                                    (((0,), (0,)), ((), ())), preferred_element_type=F32)
    st = st_ref[...]
    for c in range(n_chunks):
        lo = c * ch
        o_ref[lo:lo + ch, :] = lax.dot_general(qe[lo:lo + ch, :], st.astype(BF16),
                                               (((1,), (1,)), ((), ())), preferred_element_type=F32)
        st = st * jnp.exp(gp_ref[lo + 2 * ch - 1:lo + 2 * ch, :]) + ud_ref[c]
    st_ref[...] = st
    so_ref[...] = st

    grp = min(tb, HGRN_ROW_GROUP)
    pos = lax.broadcasted_iota(jnp.int32, (grp, 1), 0) & (ch - 1)
    for r0 in range(0, tb, grp):
        qg = qs_ref[r0:r0 + grp, :]
        gg = gp_ref[ch + r0:ch + r0 + grp, :]
        o = o_ref[r0:r0 + grp, :]
        for d in range(ch):
            lo = ch + r0 - d
            dec = jnp.exp(jnp.where(pos >= d, gg - gp_ref[lo:lo + grp, :], NEG_BIG))
            a = jnp.sum(qg * kp_ref[lo:lo + grp, :] * dec, axis=-1, keepdims=True)
            o = o + a * vp_ref[lo:lo + grp, :]
        o = o * lax.rsqrt(jnp.mean(o * o, axis=-1, keepdims=True) + RMS_EPS) * ng_ref[...]
        y_ref[r0:r0 + grp, :] = (o * _silu(g_ref[r0:r0 + grp, :])).astype(y_ref.dtype)


def _hgrn(proj3, lb, norm_g, s0_t, w, n_valid):
    b, t, _ = proj3.shape
    nh = w // A_HD
    tb = _pick(t, TIME_BLOCK_CAP, HGRN_CHUNK)
    ch = HGRN_CHUNK
    assert ch & (ch - 1) == 0
    idx = np.arange(tb)
    tri = ((idx[:, None] // ch == idx[None, :] // ch) & (idx[None, :] <= idx[:, None])).astype(np.float32)

    def col(off):
        return lambda bb, h, i: (bb, i, off * nh + h)

    return pl.pallas_call(
        functools.partial(_hgrn_kernel, tb=tb, n_valid=n_valid, masked=n_valid < t, n_chunks=tb // ch),
        out_shape=(jax.ShapeDtypeStruct((b, t, w), BF16), jax.ShapeDtypeStruct((b, nh, A_HD, A_HD), F32)),
        grid=(b, nh, t // tb),
        in_specs=[
            pl.BlockSpec((None, tb, A_HD), col(0)),
            pl.BlockSpec((None, tb, A_HD), col(1)),
            pl.BlockSpec((None, tb, A_HD), col(2)),
            pl.BlockSpec((None, tb, A_HD), col(3)),
            pl.BlockSpec((1, A_HD), lambda bb, h, i: (0, h)),
            pl.BlockSpec((1, A_HD), lambda bb, h, i: (0, 0)),
            pl.BlockSpec((tb, tb), lambda bb, h, i: (0, 0)),
            pl.BlockSpec((None, None, A_HD, A_HD), lambda bb, h, i: (bb, h, 0, 0)),
        ],
        out_specs=(
            pl.BlockSpec((None, tb, A_HD), lambda bb, h, i: (bb, i, h)),
            pl.BlockSpec((None, None, A_HD, A_HD), lambda bb, h, i: (bb, h, 0, 0)),
        ),
        scratch_shapes=[pltpu.VMEM((A_HD, A_HD), F32)] + [pltpu.VMEM((tb + ch, A_HD), F32)] * 3
        + [pltpu.VMEM((tb, A_HD), F32)] * 2 + [pltpu.VMEM((tb // ch, A_HD, A_HD), F32)],
        compiler_params=_cp("parallel", "parallel", "arbitrary"),
        name="hgrn2",
    )(proj3, proj3, proj3, proj3, lb.reshape(1, w), norm_g.reshape(1, A_HD), jnp.asarray(tri), s0_t)


def _lru_kernel(x_ref, g_ref, buf_ref, h0_ref, cw_ref, cb_ref, wa_ref, ba_ref, wx_ref, bx_ref, lam_ref,
                y_ref, hl_ref, bo_ref, xs_ref, a_ref, b_ref, hc_ref, *, tb, n_valid, masked, i_state, r_state):
    it = pl.program_id(1)
    nh = wa_ref.shape[0]

    @pl.when(it == 0)
    def _():
        xs_ref[5:8, :] = buf_ref[...]
        hc_ref[...] = h0_ref[...]

    @pl.when(it > 0)
    def _():
        xs_ref[0:8, :] = xs_ref[tb:tb + 8, :]

    x = x_ref[...]
    xs_ref[8:tb + 8, :] = x
    cw = cw_ref[...]
    xc = xs_ref[5:tb + 5, :] * cw[0:1, :]
    xc = xc + xs_ref[6:tb + 6, :] * cw[1:2, :]
    xc = xc + xs_ref[7:tb + 7, :] * cw[2:3, :]
    xc = xc + x * cw[3:4, :]
    xc = xc + cb_ref[...]
    xcb = xc.astype(BF16)
    ra, rx = [], []
    for h in range(nh):
        xh = xcb[:, h * B_HD:(h + 1) * B_HD]
        ra.append(jnp.dot(xh, wa_ref[h], preferred_element_type=F32))
        rx.append(jnp.dot(xh, wx_ref[h], preferred_element_type=F32))
    r = _sigmoid(jnp.concatenate(ra, axis=1) + ba_ref[...])
    ig = _sigmoid(jnp.concatenate(rx, axis=1) + bx_ref[...])
    log_a = (-LRU_C) * r * _softplus(-lam_ref[...])
    a = jnp.exp(log_a)
    om = -jnp.tanh(log_a) * (a * a + 1.0)
    bb = jnp.sqrt(jnp.maximum(om, 0.0)) * (ig * xc)
    if masked:
        row = it * tb + lax.broadcasted_iota(jnp.int32, (tb, 1), 0)
        live = row < n_valid
        a = jnp.where(live, a, 1.0)
        bb = jnp.where(live, bb, 0.0)
    a_ref[...] = a
    b_ref[...] = bb

    def step(t, h):
        h = a_ref[pl.ds(t, 1), :] * h + b_ref[pl.ds(t, 1), :]
        b_ref[pl.ds(t, 1), :] = h
        return h

    h = lax.fori_loop(0, tb, step, hc_ref[...], unroll=8)
    hc_ref[...] = h
    hl_ref[...] = h
    y_ref[...] = (b_ref[...] * _gelu_tanh(g_ref[...])).astype(y_ref.dtype)

    @pl.when(it == i_state)
    def _():
        bo_ref[...] = xs_ref[8 + r_state:11 + r_state, :]


def _lru(proj3, buf, h0, p, w, n_valid):
    b, t, _ = proj3.shape
    nh = w // B_HD
    tb = _pick(t, TIME_BLOCK_CAP, SUBLANE)
    i_state, r_state = divmod(n_valid - 3, tb)
    assert n_valid >= 3 and r_state + 3 <= tb
    row = lambda v: v.reshape(1, w).astype(F32)
    full = lambda shape: pl.BlockSpec(shape, lambda bb, i: (0,) * len(shape))
    return pl.pallas_call(
        functools.partial(_lru_kernel, tb=tb, n_valid=n_valid, masked=n_valid < t,
                          i_state=i_state, r_state=r_state),
        out_shape=(jax.ShapeDtypeStruct((b, t, w), BF16), jax.ShapeDtypeStruct((b, 1, w), F32),
                   jax.ShapeDtypeStruct((b, 3, w), F32)),
        grid=(b, t // tb),
        in_specs=[
            pl.BlockSpec((None, tb, w), lambda bb, i: (bb, i, 4)),
            pl.BlockSpec((None, tb, w), lambda bb, i: (bb, i, 5)),
            pl.BlockSpec((None, 3, w), lambda bb, i: (bb, 0, 0)),
            pl.BlockSpec((None, 1, w), lambda bb, i: (bb, 0, 0)),
            full((4, w)), full((1, w)), full((nh, B_HD, B_HD)), full((1, w)),
            full((nh, B_HD, B_HD)), full((1, w)), full((1, w)),
        ],
        out_specs=(
            pl.BlockSpec((None, tb, w), lambda bb, i: (bb, i, 0)),
            pl.BlockSpec((None, 1, w), lambda bb, i: (bb, 0, 0)),
            pl.BlockSpec((None, 3, w), lambda bb, i: (bb, 0, 0)),
        ),
        scratch_shapes=[pltpu.VMEM((tb + 8, w), F32), pltpu.VMEM((tb, w), F32), pltpu.VMEM((tb, w), F32),
                        pltpu.VMEM((1, w), F32)],
        compiler_params=_cp("parallel", "arbitrary"),
        name="rglru",
    )(proj3, proj3, buf, h0.reshape(b, 1, w), p['lru_conv_w'].astype(F32), row(p['lru_conv_b']),
      p['lru_wa'].astype(BF16), row(p['lru_ba']), p['lru_wx'].astype(BF16), row(p['lru_bx']),
      row(p['lru_lambda']))


def _qk_kernel(q_ref, k_ref, v_ref, qg_ref, kg_ref, c_ref, sa_ref, sb_ref,
               qo_ref, ko_ref, vo_ref, kb_ref, vb_ref):
    nh = q_ref.shape[1] // C_HD
    cosf, sina, sinb = c_ref[...], sa_ref[...], sb_ref[...]

    def prep(x, g):
        xn = x * lax.rsqrt(jnp.mean(x * x, axis=-1, keepdims=True) + RMS_EPS) * g
        up = pltpu.roll(xn, C_HD - ROPE_DIM // 2, axis=1)
        dn = pltpu.roll(xn, ROPE_DIM // 2, axis=1)
        return xn * cosf + up * sina + dn * sinb

    for h in range(nh):
        sl = slice(h * C_HD, (h + 1) * C_HD)
        qo_ref[:, sl] = (prep(q_ref[:, sl], qg_ref[...]) * (C_HD ** -0.5)).astype(qo_ref.dtype)
        kn = prep(k_ref[:, sl], kg_ref[...])
        ko_ref[:, sl] = kn
        kb_ref[:, sl] = kn.astype(kb_ref.dtype)
    v = v_ref[...]
    vo_ref[...] = v
    vb_ref[...] = v.astype(vb_ref.dtype)


def _rope_tables(pos):
    half = ROPE_DIM // 2
    inv = ROPE_THETA ** (-jnp.arange(half, dtype=F32) / half)
    ang = pos.astype(F32)[:, None] * inv[None, :]
    cos, sin = jnp.cos(ang), jnp.sin(ang)
    n = pos.shape[0]
    rest = C_HD - ROPE_DIM
    cosf = jnp.concatenate([cos, cos, jnp.ones((n, rest), F32)], axis=1)
    sina = jnp.concatenate([-sin, jnp.zeros((n, half + rest), F32)], axis=1)
    sinb = jnp.concatenate([jnp.zeros((n, half), F32), sin, jnp.zeros((n, rest), F32)], axis=1)
    return cosf, sina, sinb


def _qk_prep(proj3, qg, kg, tables, w):
    b, t, _ = proj3.shape
    tb = _pick(t, TIME_BLOCK_CAP, SUBLANE)
    tab = pl.BlockSpec((tb, C_HD), lambda bb, i: (i, 0))
    gsp = pl.BlockSpec((1, C_HD), lambda bb, i: (0, 0))
    src = lambda c: pl.BlockSpec((None, tb, w), lambda bb, i: (bb, i, c))
    dst = pl.BlockSpec((None, tb, w), lambda bb, i: (bb, i, 0))
    return pl.pallas_call(
        _qk_kernel,
        out_shape=(jax.ShapeDtypeStruct((b, t, w), BF16), jax.ShapeDtypeStruct((b, t, w), F32),
                   jax.ShapeDtypeStruct((b, t, w), F32),
                   jax.ShapeDtypeStruct((b, t, w), BF16), jax.ShapeDtypeStruct((b, t, w), BF16)),
        grid=(b, t // tb),
        in_specs=[src(6), src(7), src(8), gsp, gsp, tab, tab, tab],
        out_specs=(dst, dst, dst, dst, dst),
        compiler_params=_cp("parallel", "parallel"),
        name="qk_norm_rope",
    )(proj3, proj3, proj3, qg.reshape(1, C_HD).astype(F32), kg.reshape(1, C_HD).astype(F32), *tables)


def _pattern_bias(dist):
    dist = np.asarray(dist, np.int64)
    cnt = np.zeros(dist.shape, np.int64)
    for win, dil in C_PATTERNS:
        cnt += (dist >= 0) & (dist <= win) & (dist % dil == 0)
    return np.where(cnt > 0, np.log(np.maximum(cnt, 1)), MASK_BIAS).astype(np.float32)


def _attn_kernel(q_ref, k_ref, v_ref, bias_ref, o_ref, *, n_back, span):
    qi = pl.program_id(2)
    rows = pl.ds(pl.multiple_of(jnp.maximum(qi - n_back, 0) * ATT_BLK, ATT_BLK), span)
    s = lax.dot_general(q_ref[...], k_ref[rows, :], (((1,), (1,)), ((), ())), preferred_element_type=F32)
    s = s + bias_ref[...]
    m = jnp.max(s, axis=-1, keepdims=True)
    p = jnp.exp(s - m)
    l = jnp.sum(p, axis=-1, keepdims=True)
    acc = jnp.dot(p.astype(BF16), v_ref[rows, :], preferred_element_type=F32)
    o_ref[...] = (acc / l).astype(o_ref.dtype)


def _attn_prompt(qn, kb, vb, w):
    b, t, _ = qn.shape
    nh = w // C_HD
    blk = ATT_BLK
    assert t % blk == 0
    n_back = -(-C_WIN // blk)
    n_span = min(n_back + 1, t // blk)
    r = np.arange(blk)
    n_tab = min(n_back, n_span - 1) + 1
    wide = jnp.asarray(_pattern_bias((n_tab - 1) * blk + r[:, None] - np.arange((n_tab - 1 + n_span) * blk)[None, :]))
    bias = jnp.stack([wide[:, (n_tab - 1 - k) * blk:(n_tab - 1 - k + n_span) * blk] for k in range(n_tab)])
    return pl.pallas_call(
        functools.partial(_attn_kernel, n_back=n_back, span=n_span * blk),
        out_shape=jax.ShapeDtypeStruct((b, t, w), BF16),
        grid=(b, nh, t // blk),
        in_specs=[
            pl.BlockSpec((None, blk, C_HD), lambda bb, h, i: (bb, i, h)),
            pl.BlockSpec((None, t, C_HD), lambda bb, h, i: (bb, 0, h)),
            pl.BlockSpec((None, t, C_HD), lambda bb, h, i: (bb, 0, h)),
            pl.BlockSpec((None, blk, n_span * blk), lambda bb, h, i: (jnp.minimum(i, n_tab - 1), 0, 0)),
        ],
        out_specs=pl.BlockSpec((None, blk, C_HD), lambda bb, h, i: (bb, i, h)),
        compiler_params=_cp("parallel", "parallel", "arbitrary"),
        name="attn_prompt",
    )(qn, kb, vb, bias)


def _attn_s_kernel(q_ref, kp_ref, vp_ref, kn_ref, vn_ref, bp_ref, bn_ref, o_ref):
    q = q_ref[...]
    nt = (((1,), (1,)), ((), ()))
    sp = lax.dot_general(q, kp_ref[...].astype(BF16), nt, preferred_element_type=F32) + bp_ref[...]
    sn = lax.dot_general(q, kn_ref[...], nt, preferred_element_type=F32) + bn_ref[...]
    m = jnp.maximum(jnp.max(sp, axis=-1, keepdims=True), jnp.max(sn, axis=-1, keepdims=True))
    pp = jnp.exp(sp - m)
    pn = jnp.exp(sn - m)
    l = jnp.sum(pp, axis=-1, keepdims=True) + jnp.sum(pn, axis=-1, keepdims=True)
    acc = jnp.dot(pp.astype(BF16), vp_ref[...].astype(BF16), preferred_element_type=F32)
    acc = acc + jnp.dot(pn.astype(BF16), vn_ref[...], preferred_element_type=F32)
    o_ref[...] = (acc / l).astype(o_ref.dtype)


def _attn_sample(qn, kpast, vpast, kb, vb, w, n_valid):
    b, tp, _ = qn.shape
    lp = kpast.shape[1]
    nh = w // C_HD
    t = np.arange(tp)
    live = t < n_valid
    bp = jnp.asarray(np.where(live[:, None], _pattern_bias(lp + t[:, None] - np.arange(lp)[None, :]), 0.0),
                     F32)
    bn = jnp.asarray(np.where(live[:, None] & live[None, :], _pattern_bias(t[:, None] - t[None, :]),
                              MASK_BIAS), F32)
    new = pl.BlockSpec((None, tp, C_HD), lambda bb, h: (bb, 0, h))
    past = pl.BlockSpec((None, lp, C_HD), lambda bb, h: (bb, 0, h))
    return pl.pallas_call(
        _attn_s_kernel,
        out_shape=jax.ShapeDtypeStruct((b, tp, w), BF16),
        grid=(b, nh),
        in_specs=[new, past, past, new, new,
                  pl.BlockSpec((tp, lp), lambda bb, h: (0, 0)),
                  pl.BlockSpec((tp, tp), lambda bb, h: (0, 0))],
        out_specs=new,
        compiler_params=_cp("parallel", "parallel"),
        name="attn_sample",
    )(qn, kpast, vpast, kb, vb, bp, bn)


def _seg_sum(x, e_ref):
    nb = x.shape[1] // LANE
    e = e_ref[...]
    return jnp.concatenate(
        [jnp.dot(x[:, i * LANE:(i + 1) * LANE], e, precision=HI, preferred_element_type=F32) for i in range(nb)],
        axis=1)


def _rwkv_prep_kernel(r_ref, k_ref, v_ref, x_ref, sr_ref, sk_ref, sv_ref, sx_ref,
                      mr_ref, mk_ref, mv_ref, mx_ref, w0_ref, w2_ref, a0_ref, a2_ref, g2_ref,
                      kk_ref, ka_ref, e_ref,
                      ro_ref, wo_ref, ko_ref, vo_ref, zo_ref, bo_ref, go_ref,
                      pr_ref, pk_ref, pv_ref, px_ref, *, tb, n_valid, masked):
    it = pl.program_id(1)

    @pl.when(it == 0)
    def _():
        pr_ref[...] = sr_ref[...]
        pk_ref[...] = sk_ref[...]
        pv_ref[...] = sv_ref[...]
        px_ref[...] = sx_ref[...]

    first = lax.broadcasted_iota(jnp.int32, (tb, 1), 0) == 0

    def shifted(x_ref_, prev_ref, mu_ref):
        x = x_ref_[...]
        prev = jnp.where(first, prev_ref[...], pltpu.roll(x, 1, axis=0))
        prev_ref[...] = x[tb - 1:tb, :]
        return x + mu_ref[...] * (prev - x)

    r = shifted(r_ref, pr_ref, mr_ref)
    k = shifted(k_ref, pk_ref, mk_ref)
    v = shifted(v_ref, pv_ref, mv_ref)
    x = shifted(x_ref, px_ref, mx_ref)
    lo = x[:, 0:LANE]
    hi = x[:, LANE:]
    mw = jnp.dot(jnp.tanh(lo).astype(BF16), w2_ref[...], preferred_element_type=F32)
    ma = jnp.dot(lo.astype(BF16), a2_ref[...], preferred_element_type=F32)
    g = jnp.dot(_sigmoid(hi).astype(BF16), g2_ref[...], preferred_element_type=F32)
    log_w = -_softplus(-(w0_ref[...] + mw)) - 0.5
    decay = jnp.exp(-jnp.exp(log_w))
    a = _sigmoid(a0_ref[...] + ma)
    kk = k * kk_ref[...]
    kk = kk / jnp.maximum(jnp.sqrt(_seg_sum(kk * kk, e_ref)), 1e-12)
    k2 = k * (1.0 + (a - 1.0) * ka_ref[...])
    bvec = kk * a
    if masked:
        live = (it * tb + lax.broadcasted_iota(jnp.int32, (tb, 1), 0)) < n_valid
        decay = jnp.where(live, decay, 1.0)
        k2 = jnp.where(live, k2, 0.0)
        bvec = jnp.where(live, bvec, 0.0)
    for ref, val in ((ro_ref, r), (wo_ref, decay), (ko_ref, k2), (zo_ref, -kk), (bo_ref, bvec)):
        for c in range(ref.shape[0]):
            ref[c] = val
    vo_ref[...] = v
    go_ref[...] = g


def _rwkv_prep(proj3, shift, pw, w, xw, n_valid, ncopy):
    b, t, _ = proj3.shape
    tb = _pick(t, TIME_BLOCK_CAP, SUBLANE)
    xcol = (12 * w) // xw
    big = lambda c: pl.BlockSpec((None, tb, w), lambda bb, i: (bb, i, c))
    st = lambda n: pl.BlockSpec((None, 1, n), lambda bb, i: (bb, 0, 0))
    full = lambda shape: pl.BlockSpec(shape, lambda bb, i: (0,) * len(shape))
    one = pl.BlockSpec((None, tb, w), lambda bb, i: (bb, i, 0))
    cop = pl.BlockSpec((ncopy, None, tb, w), lambda bb, i: (0, bb, i, 0))
    one_s = jax.ShapeDtypeStruct((b, t, w), F32)
    cop_s = jax.ShapeDtypeStruct((ncopy, b, t, w), F32)
    sr, sk, sv, sx = shift
    r4, w4, k4, v3, z4, b4, g3 = pl.pallas_call(
        functools.partial(_rwkv_prep_kernel, tb=tb, n_valid=n_valid, masked=n_valid < t),
        out_shape=(cop_s, cop_s, cop_s, one_s, cop_s, cop_s, one_s),
        grid=(b, t // tb),
        in_specs=[big(9), big(10), big(11), pl.BlockSpec((None, tb, xw), lambda bb, i: (bb, i, xcol)),
                  st(w), st(w), st(w), st(xw),
                  full((1, w)), full((1, w)), full((1, w)), full((1, xw)),
                  full((1, w)), full((LANE, w)), full((1, w)), full((LANE, w)), full((xw - LANE, w)),
                  full((1, w)), full((1, w)), full((LANE, LANE))],
        out_specs=(cop, cop, cop, one, cop, cop, one),
        scratch_shapes=[pltpu.VMEM((1, w), F32)] * 3 + [pltpu.VMEM((1, xw), F32)],
        compiler_params=_cp("parallel", "arbitrary"),
        name="rwkv_prep",
    )(proj3, proj3, proj3, proj3, sr, sk, sv, sx, *pw)
    return r4, w4, k4, z4, b4, v3, g3


def _rwkv_scan_kernel(r_ref, w_ref, k_ref, z_ref, b_ref, v_ref, s0_ref, o_ref, so_ref, s_ref, *, tc):
    nk, nv = s_ref.shape[0], s_ref.shape[1]
    vg = min(nv, SCAN_V_GROUP)
    groups = [slice(g0, g0 + vg) for g0 in range(0, nv, vg)]
    kc = min(nk, SCAN_K_CHUNK)
    zero = jnp.zeros((vg, LANE), F32)

    @pl.when(pl.program_id(0) == 0)
    def _():
        s_ref[...] = s0_ref[...]

    def tree(parts):
        parts = list(parts)
        while len(parts) > 1:
            parts = [sum(parts[i:i + 2][1:], parts[i]) for i in range(0, len(parts), 2)]
        return parts[0]

    def step(t, carry):
        def row(ref, kx):
            return ref[t, pl.ds(kx, 1), :]

        def pass1(g, c, acc):
            acc = list(acc)
            for j in range(kc):
                kx = c * kc + j
                acc[j % 4] = acc[j % 4] + s_ref[kx, g, :] * row(z_ref, kx)
            return tuple(acc)

        def pass2(g, sz, vv, c, acc):
            acc = list(acc)
            for j in range(kc):
                kx = c * kc + j
                sn = s_ref[kx, g, :] * row(w_ref, kx) + sz * row(b_ref, kx) + vv * row(k_ref, kx)
                s_ref[kx, g, :] = sn
                acc[j % 4] = acc[j % 4] + sn * row(r_ref, kx)
            return tuple(acc)

        for g in groups:
            sz = tree(lax.fori_loop(0, nk // kc, functools.partial(pass1, g), (zero,) * 4))
            acc = lax.fori_loop(0, nk // kc, functools.partial(pass2, g, sz, v_ref[t, g, :]), (zero,) * 4)
            o_ref[t, g, :] = tree(acc)
        return carry

    lax.fori_loop(0, tc, step, 0)
    so_ref[...] = s_ref[...]


def _rwkv_scan(rl, wl, kl, zl, bl, vl, s0l):
    t, nk, _ = rl.shape
    nv = vl.shape[1]
    tc = _pick(t, SCAN_STEPS_CAP, 1)
    tok = pl.BlockSpec((tc, nk, LANE), lambda i: (i, 0, 0))
    vsp = pl.BlockSpec((tc, nv, LANE), lambda i: (i, 0, 0))
    ssp = pl.BlockSpec((nk, nv, LANE), lambda i: (0, 0, 0))
    return pl.pallas_call(
        functools.partial(_rwkv_scan_kernel, tc=tc),
        out_shape=(jax.ShapeDtypeStruct((t, nv, LANE), F32), jax.ShapeDtypeStruct((nk, nv, LANE), F32)),
        grid=(t // tc,),
        in_specs=[tok, tok, tok, tok, tok, vsp, ssp],
        out_specs=(vsp, ssp),
        scratch_shapes=[pltpu.VMEM((nk, nv, LANE), F32)],
        compiler_params=_cp("arbitrary"),
        name="rwkv_scan",
    )(rl, wl, kl, zl, bl, vl, s0l)


def _rwkv_post_kernel(o_ref, r_ref, k_ref, v_ref, g_ref, lg_ref, lb_ref, rk_ref, e_ref, y_ref):
    o = o_ref[...]
    inv_n = 1.0 / D_HD
    mu = _seg_sum(o, e_ref) * inv_n
    d = o - mu
    var = _seg_sum(d * d, e_ref) * inv_n
    on = d * lax.rsqrt(var + RWKV_GN_EPS) * lg_ref[...] + lb_ref[...]
    bonus = _seg_sum(r_ref[...] * k_ref[...] * rk_ref[...], e_ref) * v_ref[...]
    y_ref[...] = ((on + bonus) * g_ref[...]).astype(y_ref.dtype)


def _rwkv_post(o3, r3, k3, v3, g3, lg, lb, rk, e):
    b, t, w = o3.shape
    tb = _pick(t, TIME_BLOCK_CAP, SUBLANE)
    big = pl.BlockSpec((None, tb, w), lambda bb, i: (bb, i, 0))
    rowsp = pl.BlockSpec((1, w), lambda bb, i: (0, 0))
    return pl.pallas_call(
        _rwkv_post_kernel,
        out_shape=jax.ShapeDtypeStruct((b, t, w), BF16),
        grid=(b, t // tb),
        in_specs=[big] * 5 + [rowsp] * 3 + [pl.BlockSpec((LANE, LANE), lambda bb, i: (0, 0))],
        out_specs=big,
        compiler_params=_cp("parallel", "parallel"),
        name="rwkv_post",
    )(o3, r3, k3, v3, g3, lg, lb, rk, e)


def _rwkv(proj3, shift, s0, pw, w, xw, n_valid):
    b, t, _ = proj3.shape
    nh = w // D_HD
    chains = b * nh
    assert LANE % chains == 0 and D_HD % (LANE // chains) == 0
    vh = LANE // chains
    nv = D_HD // vh
    prep_w, (lg, lb, rk, e) = pw
    r4, w4, k4, z4, b4, v3, g3 = _rwkv_prep(proj3, shift, prep_w, w, xw, n_valid, vh)
    r3, k3 = r4[0], k4[0]

    def klay(x4):
        return x4.reshape(vh, b, t, nh, D_HD).transpose(2, 4, 0, 1, 3).reshape(t, D_HD, LANE)

    def vlay(x3):
        return x3.reshape(b, t, nh, vh, nv).transpose(1, 4, 3, 0, 2).reshape(t, nv, LANE)

    s0l = s0.reshape(b, nh, vh, nv, D_HD).transpose(4, 3, 2, 0, 1).reshape(D_HD, nv, LANE)
    ol, sl = _rwkv_scan(klay(r4), klay(w4), klay(k4), klay(z4), klay(b4), vlay(v3), s0l)
    o3 = ol.reshape(t, nv, vh, b, nh).transpose(3, 0, 4, 2, 1).reshape(b, t, w)
    s_new = sl.reshape(D_HD, nv, vh, b, nh).transpose(3, 4, 2, 1, 0).reshape(b, nh, D_HD, D_HD)
    y = _rwkv_post(o3, r3, k3, v3, g3, lg, lb, rk, e)
    return y, s_new


def _layer(x3, n_valid, tables, state, lbv, p, dims):
    d, w, xw, n_in = dims
    b, t, _ = x3.shape
    k_past, v_past, s_a, h_b, buf_b, s_d, shift_d, buf_f = state
    m = b * t
    x2 = x3.reshape(m, d)
    h = _rmsnorm_bf16(x2, p['ln_mix_g'])
    proj = _matmul(h, p['w_in'], tm_cap=MM_TM_CAP, tn_cap=1152, tk_cap=d, w_resident=True)
    proj3 = proj.reshape(b, t, -1)
    y_a, s_a_t = _hgrn(proj3, lbv, p['hgrn_norm_g'].astype(F32), jnp.swapaxes(s_a, -1, -2), w, n_valid)
    s_a_new = jnp.swapaxes(s_a_t, -1, -2)
    y_b, h_b_new, buf_b_new = _lru(proj3, buf_b, h_b, p, w, n_valid)
    qn, kn, v_new, kb, vb = _qk_prep(proj3, p['attn_q_norm_g'], p['attn_k_norm_g'], tables, w)
    if k_past is None:
        y_c = _attn_prompt(qn, kb, vb, w)
    else:
        lp = k_past.shape[1]
        y_c = _attn_sample(qn, k_past.reshape(b, lp, w), v_past.reshape(b, lp, w), kb, vb, w, n_valid)
    sh = shift_d.reshape(b, 1, -1)
    shift = (sh[:, :, 0:w], sh[:, :, w:2 * w], sh[:, :, 2 * w:3 * w],
             jnp.pad(sh[:, :, 3 * w:], ((0, 0), (0, 0), (0, xw - (sh.shape[2] - 3 * w)))))
    y_d, s_d_new = _rwkv(proj3, shift, s_d, p['rwkv'], w, xw, n_valid)
    shift_new = proj3[:, n_valid - 1, 9 * w:n_in]
    y_mix = jnp.concatenate([y_a, y_b, y_c, y_d], axis=-1).reshape(m, 4 * w)
    x2 = _matmul(y_mix, p['w_out'], x2, tm_cap=MM_TM_CAP, tn_cap=512, tk_cap=4 * w, w_resident=True)
    h = _rmsnorm_bf16(x2, p['ln_ffn_g'])
    z, buf_f_new = _ffn_gate_up(h.reshape(b, t, d), p['ffn_w_gate'], p['ffn_w_up'], p['ffn_conv_w'], buf_f, n_valid)
    f = z.shape[-1]
    x2 = _matmul(z.reshape(m, f), p['ffn_w_down'], x2, tm_cap=MM_TM_CAP, tn_cap=512, tk_cap=f // 2,
                 w_resident=False)
    keep = slice(max(n_valid - C_WIN, 0), n_valid) if k_past is None else slice(0, n_valid)
    new_state = (kn[:, keep], v_new[:, keep], s_a_new, h_b_new.reshape(b, w), buf_b_new,
                 s_d_new, shift_new, buf_f_new)
    return x2.reshape(b, t, d), new_state


def kernel(x_prompt, x_sample, cache_attn_k, cache_attn_v, state_hgrn, state_lru_h, state_lru_conv, state_rwkv, state_rwkv_shift, state_ffn_conv, ln_mix_g, w_in, hgrn_lb_logits, hgrn_norm_g, lru_conv_w, lru_conv_b, lru_wa, lru_ba, lru_wx, lru_bx, lru_lambda, attn_q_norm_g, attn_k_norm_g, rwkv_mu, rwkv_w0, rwkv_w2, rwkv_a0, rwkv_a2, rwkv_g2, rwkv_k_k, rwkv_k_a, rwkv_r_k, rwkv_ln_g, rwkv_ln_b, w_out, ln_ffn_g, ffn_w_gate, ffn_w_up, ffn_conv_w, ffn_w_down):
    bp, tp, d = x_prompt.shape
    bs, ts, _ = x_sample.shape
    depth = w_in.shape[0]
    n_in = w_in.shape[2]
    w = d // 4
    n_pad = -(-n_in // LANE) * LANE
    rank_w, rank_a, rank_g = rwkv_w2.shape[1], rwkv_a2.shape[1], rwkv_g2.shape[1]
    assert rank_w + rank_a == LANE
    xw = n_pad - 12 * w
    dims = (d, w, xw, n_in)
    row = lambda v: v.reshape(1, -1).astype(F32)

    lb_soft = jax.nn.softmax(hgrn_lb_logits.astype(F32), axis=0)
    lb_all = jnp.cumsum(lb_soft, axis=0) - lb_soft[:1]

    w_in_b = jnp.pad(w_in, ((0, 0), (0, 0), (0, n_pad - n_in))).astype(BF16)
    w_out_b = w_out.astype(BF16)
    wg_b, wu_b, wd_b = ffn_w_gate.astype(BF16), ffn_w_up.astype(BF16), ffn_w_down.astype(BF16)
    zeros = lambda r: jnp.zeros((r, w), F32)
    head_id = np.arange(LANE) // D_HD
    e_blk = jnp.asarray((head_id[:, None] == head_id[None, :]).astype(np.float32))

    layers = []
    for l in range(depth):
        mu = rwkv_mu[l].astype(F32)
        prep_w = (
            row(mu[0:w]), row(mu[w:2 * w]), row(mu[2 * w:3 * w]),
            row(jnp.pad(mu[3 * w:], (0, xw - (mu.shape[0] - 3 * w)))),
            row(rwkv_w0[l]),
            jnp.concatenate([rwkv_w2[l].astype(F32), zeros(rank_a)], axis=0).astype(BF16),
            row(rwkv_a0[l]),
            jnp.concatenate([zeros(rank_w), rwkv_a2[l].astype(F32)], axis=0).astype(BF16),
            jnp.concatenate([rwkv_g2[l].astype(F32), zeros(xw - LANE - rank_g)], axis=0).astype(BF16),
            row(rwkv_k_k[l]), row(rwkv_k_a[l]), e_blk,
        )
        post_w = (row(rwkv_ln_g[l]), row(rwkv_ln_b[l]), row(rwkv_r_k[l]), e_blk)
        layers.append({
            'ln_mix_g': ln_mix_g[l], 'w_in': w_in_b[l], 'hgrn_norm_g': hgrn_norm_g[l],
            'lru_conv_w': lru_conv_w[l], 'lru_conv_b': lru_conv_b[l], 'lru_wa': lru_wa[l],
            'lru_ba': lru_ba[l], 'lru_wx': lru_wx[l], 'lru_bx': lru_bx[l], 'lru_lambda': lru_lambda[l],
            'attn_q_norm_g': attn_q_norm_g[l], 'attn_k_norm_g': attn_k_norm_g[l],
            'rwkv': (prep_w, post_w), 'w_out': w_out_b[l], 'ln_ffn_g': ln_ffn_g[l],
            'ffn_w_gate': wg_b[l], 'ffn_w_up': wu_b[l], 'ffn_conv_w': ffn_conv_w[l].astype(F32),
            'ffn_w_down': wd_b[l],
        })

    f = ffn_w_gate.shape[2]
    nh_a, nh_d = w // A_HD, w // D_HD
    zero_state = (None, None, jnp.zeros((bp, nh_a, A_HD, A_HD), F32), jnp.zeros((bp, w), F32),
                  jnp.zeros((bp, 3, w), F32), jnp.zeros((bp, nh_d, D_HD, D_HD), F32),
                  jnp.zeros((bp, n_in - 9 * w), F32), jnp.zeros((bp, 2, f), F32))
    tab_p = _rope_tables(jnp.arange(tp, dtype=jnp.int32))
    tab_s = _rope_tables(PAST_LEN + jnp.arange(PAD_T, dtype=jnp.int32))
    xp = x_prompt.astype(F32)
    xs = jnp.pad(x_sample.astype(F32), ((0, 0), (0, PAD_T - ts), (0, 0)))
    p_states, s_states = [], []
    for l in range(depth):
        xp, st_p = _layer(xp, tp, tab_p, zero_state, lb_all[l], layers[l], dims)
        p_states.append(st_p)
        s_in = (cache_attn_k[l].astype(F32), cache_attn_v[l].astype(F32), state_hgrn[l].astype(F32),
                state_lru_h[l].astype(F32), state_lru_conv[l].astype(F32), state_rwkv[l].astype(F32),
                state_rwkv_shift[l].astype(F32), state_ffn_conv[l].astype(F32))
        xs, st_s = _layer(xs, ts, tab_s, s_in, lb_all[l], layers[l], dims)
        s_states.append(st_s)

    dp, ds = x_prompt.dtype, x_sample.dtype
    nh_c = w // C_HD

    def stack(states, i, dt, shape=None):
        out = jnp.stack([st[i] for st in states]).astype(dt)
        return out if shape is None else out.reshape((depth,) + shape)

    outs = [xp.astype(dp), xs[:, :ts].astype(ds)]
    for states, bb, tt, dt in ((p_states, bp, min(C_WIN, tp), dp), (s_states, bs, ts, ds)):
        outs += [stack(states, 0, dt, (bb, tt, nh_c, C_HD)), stack(states, 1, dt, (bb, tt, nh_c, C_HD)),
                 stack(states, 2, dt), stack(states, 3, dt), stack(states, 4, dt), stack(states, 5, dt),
                 stack(states, 6, dt), stack(states, 7, dt)]
    return tuple(outs)
```

```python
import functools
import math

import numpy as np
import jax
import jax.numpy as jnp
from jax import lax
from jax.experimental import pallas as pl
from jax.experimental.pallas import tpu as pltpu

F32 = jnp.float32
BF16 = jnp.bfloat16

PAST_LEN = 8192
A_HD = 128
B_HD = 128
C_HD = 128
D_HD = 64
ROPE_DIM = C_HD // 4
ROPE_THETA = 500000.0
C_PATTERNS = ((128, 1), (512, 4), (2048, 16))
LRU_C = 8.0
RMS_EPS = 1e-6
RWKV_GN_EPS = 64e-5
NEG_BIG = -30000.0
F_FLOOR = 1e-20

LANE = 128
SUBLANE = 8
VMEM_LIMIT_BYTES = 56 * 2**20

C_WIN = max(win for win, _ in C_PATTERNS)

TIME_BLOCK_CAP = 256
MM_TM_CAP = 1024
FFN_TN = 256
FFN_ROW_PIECES = 2
SCAN_STEPS_CAP = 32
SCAN_V_GROUP = 32
SCAN_K_CHUNK = 32

HGRN_CHUNK = 16
HGRN_ROW_GROUP = 64
PAD_T = 16
ATT_BLK = 256
MASK_BIAS = -1e30
HI = lax.Precision.HIGHEST


def _cp(*sem):
    return pltpu.CompilerParams(dimension_semantics=sem, vmem_limit_bytes=VMEM_LIMIT_BYTES)


def _pick(n, cap, quantum):
    if n <= cap:
        return n
    t = (cap // quantum) * quantum
    while t >= quantum:
        if n % t == 0:
            return t
        t -= quantum
    raise ValueError(f"no tile for {n} (cap {cap}, quantum {quantum})")


def _sigmoid(x):
    return jax.nn.sigmoid(x)


def _silu(x):
    return x * jax.nn.sigmoid(x)


def _softplus(x):
    return jnp.maximum(x, 0.0) + jnp.log(1.0 + jnp.exp(-jnp.abs(x)))


def _gelu_tanh(x):
    return x * (0.5 * (1.0 + jnp.tanh(math.sqrt(2.0 / math.pi) * (x + 0.044715 * (x * x * x)))))


def _norm_kernel(x_ref, g_ref, o_ref):
    x = x_ref[...]
    inv = lax.rsqrt(jnp.mean(x * x, axis=-1, keepdims=True) + RMS_EPS)
    o_ref[...] = (x * inv * g_ref[...]).astype(o_ref.dtype)


def _rmsnorm_bf16(x2, g):
    m, d = x2.shape
    tm = _pick(m, 256, SUBLANE)
    return pl.pallas_call(
        _norm_kernel,
        out_shape=jax.ShapeDtypeStruct((m, d), BF16),
        grid=(m // tm,),
        in_specs=[pl.BlockSpec((tm, d), lambda i: (i, 0)), pl.BlockSpec((1, d), lambda i: (0, 0))],
        out_specs=pl.BlockSpec((tm, d), lambda i: (i, 0)),
        compiler_params=_cp("parallel"),
        name="rmsnorm",
    )(x2, g.reshape(1, d).astype(F32))


def _mm_kernel(*refs, nk, has_res):
    if has_res:
        a_ref, w_ref, r_ref, o_ref = refs[:4]
    else:
        a_ref, w_ref, o_ref = refs[:3]
        r_ref = None
    part = jnp.dot(a_ref[...], w_ref[...], preferred_element_type=F32)
    if nk == 1:
        o_ref[...] = (r_ref[...] + part) if has_res else part
        return
    acc_ref = refs[-1]
    k = pl.program_id(2)

    @pl.when(k == 0)
    def _():
        acc_ref[...] = part

    @pl.when(k > 0)
    def _():
        acc_ref[...] += part

    @pl.when(k == nk - 1)
    def _():
        o_ref[...] = (r_ref[...] + acc_ref[...]) if has_res else acc_ref[...]


def _matmul(a, w, res=None, *, tm_cap, tn_cap, tk_cap, w_resident):
    m, k = a.shape
    n = w.shape[1]
    tm = _pick(m, tm_cap, SUBLANE)
    tn = _pick(n, tn_cap, LANE)
    tk = _pick(k, tk_cap, LANE)
    nk = k // tk
    if w_resident:
        grid = (n // tn, m // tm, nk)
        amap = lambda j, i, kk: (i, kk)
        wmap = lambda j, i, kk: (kk, j)
        omap = lambda j, i, kk: (i, j)
    else:
        grid = (m // tm, n // tn, nk)
        amap = lambda i, j, kk: (i, kk)
        wmap = lambda i, j, kk: (kk, j)
        omap = lambda i, j, kk: (i, j)
    in_specs = [pl.BlockSpec((tm, tk), amap), pl.BlockSpec((tk, tn), wmap)]
    args = [a, w]
    if res is not None:
        in_specs.append(pl.BlockSpec((tm, tn), omap))
        args.append(res)
    scratch = [pltpu.VMEM((tm, tn), F32)] if nk > 1 else []
    return pl.pallas_call(
        functools.partial(_mm_kernel, nk=nk, has_res=res is not None),
        out_shape=jax.ShapeDtypeStruct((m, n), F32),
        grid=grid,
        in_specs=in_specs,
        out_specs=pl.BlockSpec((tm, tn), omap),
        scratch_shapes=scratch,
        compiler_params=_cp("parallel", "parallel", "arbitrary"),
        name="matmul_res" if res is not None else "matmul",
    )(*args)


def _ffn_kernel(h_ref, wg_ref, wu_ref, cw_ref, st_ref, z_ref, so_ref, gs_ref, *, nseq, ts, nsub, i_state,
                r_state):
    i = pl.program_id(2)

    @pl.when(i == 0)
    def _():
        for s in range(nseq):
            gs_ref[s, 6:8, :] = st_ref[s]

    @pl.when(i > 0)
    def _():
        for s in range(nseq):
            gs_ref[s, 0:8, :] = gs_ref[s, ts:ts + 8, :]

    cw = cw_ref[...]
    piece = ts // nsub
    for s in range(nseq):
        for r in range(nsub):
            lo = r * piece
            rows = slice(s * ts + lo, s * ts + lo + piece)
            h = h_ref[rows, :]
            g = jnp.dot(h, wg_ref[...], preferred_element_type=F32)
            up = jnp.dot(h, wu_ref[...], preferred_element_type=F32)
            gs_ref[s, 8 + lo:8 + lo + piece, :] = g
            y = gs_ref[s, 6 + lo:6 + lo + piece, :] * cw[0:1, :]
            y = y + gs_ref[s, 7 + lo:7 + lo + piece, :] * cw[1:2, :]
            y = y + g * cw[2:3, :]
            z_ref[rows, :] = (_silu(y) * up).astype(z_ref.dtype)

    @pl.when(i == i_state)
    def _():
        for s in range(nseq):
            so_ref[s] = gs_ref[s, 8 + r_state:10 + r_state, :]


def _ffn_gate_up(h3, wg, wu, cw, st, n_valid):
    b, t, d = h3.shape
    f = wg.shape[1]
    tn = _pick(f, FFN_TN, LANE)
    if b * t <= MM_TM_CAP:
        nseq, ts, nb, nt = b, t, 1, 1
    else:
        nseq, ts, nb, nt = 1, _pick(t, MM_TM_CAP, SUBLANE), b, t // _pick(t, MM_TM_CAP, SUBLANE)
    tm = nseq * ts
    i_state, r_state = divmod(n_valid - 2, ts)
    assert r_state + 2 <= ts
    z, so = pl.pallas_call(
        functools.partial(_ffn_kernel, nseq=nseq, ts=ts, nsub=FFN_ROW_PIECES if nseq == 1 and ts % 32 == 0 else 1,
                          i_state=i_state, r_state=r_state),
        out_shape=(jax.ShapeDtypeStruct((nb, nt * tm, f), BF16), jax.ShapeDtypeStruct((b, 2, f), F32)),
        grid=(f // tn, nb, nt),
        in_specs=[
            pl.BlockSpec((None, tm, d), lambda j, bb, i: (bb, i, 0)),
            pl.BlockSpec((d, tn), lambda j, bb, i: (0, j)),
            pl.BlockSpec((d, tn), lambda j, bb, i: (0, j)),
            pl.BlockSpec((3, tn), lambda j, bb, i: (0, j)),
            pl.BlockSpec((nseq, 2, tn), lambda j, bb, i: (bb, 0, j)),
        ],
        out_specs=(
            pl.BlockSpec((None, tm, tn), lambda j, bb, i: (bb, i, j)),
            pl.BlockSpec((nseq, 2, tn), lambda j, bb, i: (bb, 0, j)),
        ),
        scratch_shapes=[pltpu.VMEM((nseq, ts + 8, tn), F32)],
        compiler_params=_cp("parallel", "parallel", "arbitrary"),
        name="ffn_gate_up",
    )(h3.reshape(nb, nt * tm, d), wg, wu, cw, st)
    return z.reshape(b, t, f), so


def _hgrn_kernel(q_ref, f_ref, i_ref, g_ref, lb_ref, ng_ref, tri_ref, s0_ref, y_ref, so_ref,
                 st_ref, kp_ref, gp_ref, vp_ref, o_ref, qs_ref, ud_ref, *, tb, n_valid, masked, n_chunks):
    it = pl.program_id(2)
    ch = HGRN_CHUNK

    @pl.when(it == 0)
    def _():
        st_ref[...] = s0_ref[...]

    lb = lb_ref[...]
    fl = f_ref[...]
    f = lb + (1.0 - lb) * _sigmoid(fl)
    lf = jnp.log(jnp.maximum(f, F_FLOOR))
    kk = (1.0 - lb) * _sigmoid(-fl)
    row = lax.broadcasted_iota(jnp.int32, (tb, 1), 0)
    if masked:
        live = (it * tb + row) < n_valid
        lf = jnp.where(live, lf, 0.0)
        kk = jnp.where(live, kk, 0.0)
    q = _silu(q_ref[...]) * (A_HD ** -0.5)
    v = i_ref[...]
    g = jnp.dot(tri_ref[...], lf, precision=HI, preferred_element_type=F32)
    zero = jnp.zeros((ch, A_HD), F32)
    for ref, val in ((kp_ref, kk), (gp_ref, g), (vp_ref, v)):
        ref[0:ch, :] = zero
        ref[ch:tb + ch, :] = val
    qs_ref[...] = q
    qe = (q * jnp.exp(g)).astype(BF16)

    for c in range(n_chunks):
        lo = c * ch
        gl = gp_ref[lo + 2 * ch - 1:lo + 2 * ch, :]
        kt = kk[lo:lo + ch, :] * jnp.exp(gl - g[lo:lo + ch, :])
        ud_ref[c] = lax.dot_general(v[lo:lo + ch, :].astype(BF16), kt.astype(BF16),
                                    (((0,), (0,)), ((), ())), preferred_element_type=F32)
    st = st_ref[...]
    for c in range(n_chunks):
        lo = c * ch
        o_ref[lo:lo + ch, :] = lax.dot_general(qe[lo:lo + ch, :], st.astype(BF16),
                                               (((1,), (1,)), ((), ())), preferred_element_type=F32)
        st = st * jnp.exp(gp_ref[lo + 2 * ch - 1:lo + 2 * ch, :]) + ud_ref[c]
    st_ref[...] = st
    so_ref[...] = st

    grp = min(tb, HGRN_ROW_GROUP)
    pos = lax.broadcasted_iota(jnp.int32, (grp, 1), 0) & (ch - 1)
    for r0 in range(0, tb, grp):
        qg = qs_ref[r0:r0 + grp, :]
        gg = gp_ref[ch + r0:ch + r0 + grp, :]
        o = o_ref[r0:r0 + grp, :]
        for d in range(ch):
            lo = ch + r0 - d
            dec = jnp.exp(jnp.where(pos >= d, gg - gp_ref[lo:lo + grp, :], NEG_BIG))
            a = jnp.sum(qg * kp_ref[lo:lo + grp, :] * dec, axis=-1, keepdims=True)
            o = o + a * vp_ref[lo:lo + grp, :]
        o = o * lax.rsqrt(jnp.mean(o * o, axis=-1, keepdims=True) + RMS_EPS) * ng_ref[...]
        y_ref[r0:r0 + grp, :] = (o * _silu(g_ref[r0:r0 + grp, :])).astype(y_ref.dtype)


def _hgrn(proj3, lb, norm_g, s0_t, w, n_valid):
    b, t, _ = proj3.shape
    nh = w // A_HD
    tb = _pick(t, TIME_BLOCK_CAP, HGRN_CHUNK)
    ch = HGRN_CHUNK
    assert ch & (ch - 1) == 0
    idx = np.arange(tb)
    tri = ((idx[:, None] // ch == idx[None, :] // ch) & (idx[None, :] <= idx[:, None])).astype(np.float32)

    def col(off):
        return lambda bb, h, i: (bb, i, off * nh + h)

    return pl.pallas_call(
        functools.partial(_hgrn_kernel, tb=tb, n_valid=n_valid, masked=n_valid < t, n_chunks=tb // ch),
        out_shape=(jax.ShapeDtypeStruct((b, t, w), BF16), jax.ShapeDtypeStruct((b, nh, A_HD, A_HD), F32)),
        grid=(b, nh, t // tb),
        in_specs=[
            pl.BlockSpec((None, tb, A_HD), col(0)),
            pl.BlockSpec((None, tb, A_HD), col(1)),
            pl.BlockSpec((None, tb, A_HD), col(2)),
            pl.BlockSpec((None, tb, A_HD), col(3)),
            pl.BlockSpec((1, A_HD), lambda bb, h, i: (0, h)),
            pl.BlockSpec((1, A_HD), lambda bb, h, i: (0, 0)),
            pl.BlockSpec((tb, tb), lambda bb, h, i: (0, 0)),
            pl.BlockSpec((None, None, A_HD, A_HD), lambda bb, h, i: (bb, h, 0, 0)),
        ],
        out_specs=(
            pl.BlockSpec((None, tb, A_HD), lambda bb, h, i: (bb, i, h)),
            pl.BlockSpec((None, None, A_HD, A_HD), lambda bb, h, i: (bb, h, 0, 0)),
        ),
        scratch_shapes=[pltpu.VMEM((A_HD, A_HD), F32)] + [pltpu.VMEM((tb + ch, A_HD), F32)] * 3
        + [pltpu.VMEM((tb, A_HD), F32)] * 2 + [pltpu.VMEM((tb // ch, A_HD, A_HD), F32)],
        compiler_params=_cp("parallel", "parallel", "arbitrary"),
        name="hgrn2",
    )(proj3, proj3, proj3, proj3, lb.reshape(1, w), norm_g.reshape(1, A_HD), jnp.asarray(tri), s0_t)


def _lru_kernel(x_ref, g_ref, buf_ref, h0_ref, cw_ref, cb_ref, wa_ref, ba_ref, wx_ref, bx_ref, lam_ref,
                y_ref, hl_ref, bo_ref, xs_ref, a_ref, b_ref, hc_ref, *, tb, n_valid, masked, i_state, r_state):
    it = pl.program_id(1)
    nh = wa_ref.shape[0]

    @pl.when(it == 0)
    def _():
        xs_ref[5:8, :] = buf_ref[...]
        hc_ref[...] = h0_ref[...]

    @pl.when(it > 0)
    def _():
        xs_ref[0:8, :] = xs_ref[tb:tb + 8, :]

    x = x_ref[...]
    xs_ref[8:tb + 8, :] = x
    cw = cw_ref[...]
    xc = xs_ref[5:tb + 5, :] * cw[0:1, :]
    xc = xc + xs_ref[6:tb + 6, :] * cw[1:2, :]
    xc = xc + xs_ref[7:tb + 7, :] * cw[2:3, :]
    xc = xc + x * cw[3:4, :]
    xc = xc + cb_ref[...]
    xcb = xc.astype(BF16)
    ra, rx = [], []
    for h in range(nh):
        xh = xcb[:, h * B_HD:(h + 1) * B_HD]
        ra.append(jnp.dot(xh, wa_ref[h], preferred_element_type=F32))
        rx.append(jnp.dot(xh, wx_ref[h], preferred_element_type=F32))
    r = _sigmoid(jnp.concatenate(ra, axis=1) + ba_ref[...])
    ig = _sigmoid(jnp.concatenate(rx, axis=1) + bx_ref[...])
    log_a = (-LRU_C) * r * _softplus(-lam_ref[...])
    a = jnp.exp(log_a)
    om = -jnp.tanh(log_a) * (a * a + 1.0)
    bb = jnp.sqrt(jnp.maximum(om, 0.0)) * (ig * xc)
    if masked:
        row = it * tb + lax.broadcasted_iota(jnp.int32, (tb, 1), 0)
        live = row < n_valid
        a = jnp.where(live, a, 1.0)
        bb = jnp.where(live, bb, 0.0)
    a_ref[...] = a
    b_ref[...] = bb

    def step(t, h):
        h = a_ref[pl.ds(t, 1), :] * h + b_ref[pl.ds(t, 1), :]
        b_ref[pl.ds(t, 1), :] = h
        return h

    h = lax.fori_loop(0, tb, step, hc_ref[...], unroll=8)
    hc_ref[...] = h
    hl_ref[...] = h
    y_ref[...] = (b_ref[...] * _gelu_tanh(g_ref[...])).astype(y_ref.dtype)

    @pl.when(it == i_state)
    def _():
        bo_ref[...] = xs_ref[8 + r_state:11 + r_state, :]


def _lru(proj3, buf, h0, p, w, n_valid):
    b, t, _ = proj3.shape
    nh = w // B_HD
    tb = _pick(t, TIME_BLOCK_CAP, SUBLANE)
    i_state, r_state = divmod(n_valid - 3, tb)
    assert n_valid >= 3 and r_state + 3 <= tb
    row = lambda v: v.reshape(1, w).astype(F32)
    full = lambda shape: pl.BlockSpec(shape, lambda bb, i: (0,) * len(shape))
    return pl.pallas_call(
        functools.partial(_lru_kernel, tb=tb, n_valid=n_valid, masked=n_valid < t,
                          i_state=i_state, r_state=r_state),
        out_shape=(jax.ShapeDtypeStruct((b, t, w), BF16), jax.ShapeDtypeStruct((b, 1, w), F32),
                   jax.ShapeDtypeStruct((b, 3, w), F32)),
        grid=(b, t // tb),
        in_specs=[
            pl.BlockSpec((None, tb, w), lambda bb, i: (bb, i, 4)),
            pl.BlockSpec((None, tb, w), lambda bb, i: (bb, i, 5)),
            pl.BlockSpec((None, 3, w), lambda bb, i: (bb, 0, 0)),
            pl.BlockSpec((None, 1, w), lambda bb, i: (bb, 0, 0)),
            full((4, w)), full((1, w)), full((nh, B_HD, B_HD)), full((1, w)),
            full((nh, B_HD, B_HD)), full((1, w)), full((1, w)),
        ],
        out_specs=(
            pl.BlockSpec((None, tb, w), lambda bb, i: (bb, i, 0)),
            pl.BlockSpec((None, 1, w), lambda bb, i: (bb, 0, 0)),
            pl.BlockSpec((None, 3, w), lambda bb, i: (bb, 0, 0)),
        ),
        scratch_shapes=[pltpu.VMEM((tb + 8, w), F32), pltpu.VMEM((tb, w), F32), pltpu.VMEM((tb, w), F32),
                        pltpu.VMEM((1, w), F32)],
        compiler_params=_cp("parallel", "arbitrary"),
        name="rglru",
    )(proj3, proj3, buf, h0.reshape(b, 1, w), p['lru_conv_w'].astype(F32), row(p['lru_conv_b']),
      p['lru_wa'].astype(BF16), row(p['lru_ba']), p['lru_wx'].astype(BF16), row(p['lru_bx']),
      row(p['lru_lambda']))


def _qk_kernel(q_ref, k_ref, v_ref, qg_ref, kg_ref, c_ref, sa_ref, sb_ref,
               qo_ref, ko_ref, vo_ref, kb_ref, vb_ref):
    nh = q_ref.shape[1] // C_HD
    cosf, sina, sinb = c_ref[...], sa_ref[...], sb_ref[...]

    def prep(x, g):
        xn = x * lax.rsqrt(jnp.mean(x * x, axis=-1, keepdims=True) + RMS_EPS) * g
        up = pltpu.roll(xn, C_HD - ROPE_DIM // 2, axis=1)
        dn = pltpu.roll(xn, ROPE_DIM // 2, axis=1)
        return xn * cosf + up * sina + dn * sinb

    for h in range(nh):
        sl = slice(h * C_HD, (h + 1) * C_HD)
        qo_ref[:, sl] = (prep(q_ref[:, sl], qg_ref[...]) * (C_HD ** -0.5)).astype(qo_ref.dtype)
        kn = prep(k_ref[:, sl], kg_ref[...])
        ko_ref[:, sl] = kn
        kb_ref[:, sl] = kn.astype(kb_ref.dtype)
    v = v_ref[...]
    vo_ref[...] = v
    vb_ref[...] = v.astype(vb_ref.dtype)


def _rope_tables(pos):
    half = ROPE_DIM // 2
    inv = ROPE_THETA ** (-jnp.arange(half, dtype=F32) / half)
    ang = pos.astype(F32)[:, None] * inv[None, :]
    cos, sin = jnp.cos(ang), jnp.sin(ang)
    n = pos.shape[0]
    rest = C_HD - ROPE_DIM
    cosf = jnp.concatenate([cos, cos, jnp.ones((n, rest), F32)], axis=1)
    sina = jnp.concatenate([-sin, jnp.zeros((n, half + rest), F32)], axis=1)
    sinb = jnp.concatenate([jnp.zeros((n, half), F32), sin, jnp.zeros((n, rest), F32)], axis=1)
    return cosf, sina, sinb


def _qk_prep(proj3, qg, kg, tables, w):
    b, t, _ = proj3.shape
    tb = _pick(t, TIME_BLOCK_CAP, SUBLANE)
    tab = pl.BlockSpec((tb, C_HD), lambda bb, i: (i, 0))
    gsp = pl.BlockSpec((1, C_HD), lambda bb, i: (0, 0))
    src = lambda c: pl.BlockSpec((None, tb, w), lambda bb, i: (bb, i, c))
    dst = pl.BlockSpec((None, tb, w), lambda bb, i: (bb, i, 0))
    return pl.pallas_call(
        _qk_kernel,
        out_shape=(jax.ShapeDtypeStruct((b, t, w), BF16), jax.ShapeDtypeStruct((b, t, w), F32),
                   jax.ShapeDtypeStruct((b, t, w), F32),
                   jax.ShapeDtypeStruct((b, t, w), BF16), jax.ShapeDtypeStruct((b, t, w), BF16)),
        grid=(b, t // tb),
        in_specs=[src(6), src(7), src(8), gsp, gsp, tab, tab, tab],
        out_specs=(dst, dst, dst, dst, dst),
        compiler_params=_cp("parallel", "parallel"),
        name="qk_norm_rope",
    )(proj3, proj3, proj3, qg.reshape(1, C_HD).astype(F32), kg.reshape(1, C_HD).astype(F32), *tables)


def _pattern_bias(dist):
    dist = np.asarray(dist, np.int64)
    cnt = np.zeros(dist.shape, np.int64)
    for win, dil in C_PATTERNS:
        cnt += (dist >= 0) & (dist <= win) & (dist % dil == 0)
    return np.where(cnt > 0, np.log(np.maximum(cnt, 1)), MASK_BIAS).astype(np.float32)


def _attn_kernel(q_ref, k_ref, v_ref, bias_ref, o_ref, *, n_back, span):
    qi = pl.program_id(2)
    rows = pl.ds(pl.multiple_of(jnp.maximum(qi - n_back, 0) * ATT_BLK, ATT_BLK), span)
    s = lax.dot_general(q_ref[...], k_ref[rows, :], (((1,), (1,)), ((), ())), preferred_element_type=F32)
    s = s + bias_ref[...]
    m = jnp.max(s, axis=-1, keepdims=True)
    p = jnp.exp(s - m)
    l = jnp.sum(p, axis=-1, keepdims=True)
    acc = jnp.dot(p.astype(BF16), v_ref[rows, :], preferred_element_type=F32)
    o_ref[...] = (acc / l).astype(o_ref.dtype)


def _attn_prompt(qn, kb, vb, w):
    b, t, _ = qn.shape
    nh = w // C_HD
    blk = ATT_BLK
    assert t % blk == 0
    n_back = -(-C_WIN // blk)
    n_span = min(n_back + 1, t // blk)
    r = np.arange(blk)
    n_tab = min(n_back, n_span - 1) + 1
    wide = jnp.asarray(_pattern_bias((n_tab - 1) * blk + r[:, None] - np.arange((n_tab - 1 + n_span) * blk)[None, :]))
    bias = jnp.stack([wide[:, (n_tab - 1 - k) * blk:(n_tab - 1 - k + n_span) * blk] for k in range(n_tab)])
    return pl.pallas_call(
        functools.partial(_attn_kernel, n_back=n_back, span=n_span * blk),
        out_shape=jax.ShapeDtypeStruct((b, t, w), BF16),
        grid=(b, nh, t // blk),
        in_specs=[
            pl.BlockSpec((None, blk, C_HD), lambda bb, h, i: (bb, i, h)),
            pl.BlockSpec((None, t, C_HD), lambda bb, h, i: (bb, 0, h)),
            pl.BlockSpec((None, t, C_HD), lambda bb, h, i: (bb, 0, h)),
            pl.BlockSpec((None, blk, n_span * blk), lambda bb, h, i: (jnp.minimum(i, n_tab - 1), 0, 0)),
        ],
        out_specs=pl.BlockSpec((None, blk, C_HD), lambda bb, h, i: (bb, i, h)),
        compiler_params=_cp("parallel", "parallel", "arbitrary"),
        name="attn_prompt",
    )(qn, kb, vb, bias)


def _attn_s_kernel(q_ref, kp_ref, vp_ref, kn_ref, vn_ref, bp_ref, bn_ref, o_ref):
    q = q_ref[...]
    nt = (((1,), (1,)), ((), ()))
    sp = lax.dot_general(q, kp_ref[...].astype(BF16), nt, preferred_element_type=F32) + bp_ref[...]
    sn = lax.dot_general(q, kn_ref[...], nt, preferred_element_type=F32) + bn_ref[...]
    m = jnp.maximum(jnp.max(sp, axis=-1, keepdims=True), jnp.max(sn, axis=-1, keepdims=True))
    pp = jnp.exp(sp - m)
    pn = jnp.exp(sn - m)
    l = jnp.sum(pp, axis=-1, keepdims=True) + jnp.sum(pn, axis=-1, keepdims=True)
    acc = jnp.dot(pp.astype(BF16), vp_ref[...].astype(BF16), preferred_element_type=F32)
    acc = acc + jnp.dot(pn.astype(BF16), vn_ref[...], preferred_element_type=F32)
    o_ref[...] = (acc / l).astype(o_ref.dtype)


def _attn_sample(qn, kpast, vpast, layer, kb, vb, w, n_valid):
    b, tp, _ = qn.shape
    lp = kpast.shape[2]
    nh = w // C_HD
    t = np.arange(tp)
    live = t < n_valid
    bp = jnp.asarray(np.where(live[:, None], _pattern_bias(lp + t[:, None] - np.arange(lp)[None, :]), 0.0),
                     F32)
    bn = jnp.asarray(np.where(live[:, None] & live[None, :], _pattern_bias(t[:, None] - t[None, :]),
                              MASK_BIAS), F32)
    new = pl.BlockSpec((None, tp, C_HD), lambda bb, h: (bb, 0, h))
    past = pl.BlockSpec((None, None, lp, C_HD), lambda bb, h: (layer, bb, 0, h))
    return pl.pallas_call(
        _attn_s_kernel,
        out_shape=jax.ShapeDtypeStruct((b, tp, w), BF16),
        grid=(b, nh),
        in_specs=[new, past, past, new, new,
                  pl.BlockSpec((tp, lp), lambda bb, h: (0, 0)),
                  pl.BlockSpec((tp, tp), lambda bb, h: (0, 0))],
        out_specs=new,
        compiler_params=_cp("parallel", "parallel"),
        name="attn_sample",
    )(qn, kpast, vpast, kb, vb, bp, bn)


def _seg_sum(x, e_ref):
    nb = x.shape[1] // LANE
    e = e_ref[...]
    return jnp.concatenate(
        [jnp.dot(x[:, i * LANE:(i + 1) * LANE], e, precision=HI, preferred_element_type=F32) for i in range(nb)],
        axis=1)


def _rwkv_prep_kernel(r_ref, k_ref, v_ref, x_ref, sr_ref, sk_ref, sv_ref, sx_ref,
                      mr_ref, mk_ref, mv_ref, mx_ref, w0_ref, w2_ref, a0_ref, a2_ref, g2_ref,
                      kk_ref, ka_ref, e_ref,
                      ro_ref, wo_ref, ko_ref, vo_ref, zo_ref, bo_ref, go_ref,
                      pr_ref, pk_ref, pv_ref, px_ref, *, tb, n_valid, masked):
    it = pl.program_id(1)

    @pl.when(it == 0)
    def _():
        pr_ref[...] = sr_ref[...]
        pk_ref[...] = sk_ref[...]
        pv_ref[...] = sv_ref[...]
        px_ref[...] = sx_ref[...]

    first = lax.broadcasted_iota(jnp.int32, (tb, 1), 0) == 0

    def shifted(x_ref_, prev_ref, mu_ref):
        x = x_ref_[...]
        prev = jnp.where(first, prev_ref[...], pltpu.roll(x, 1, axis=0))
        prev_ref[...] = x[tb - 1:tb, :]
        return x + mu_ref[...] * (prev - x)

    r = shifted(r_ref, pr_ref, mr_ref)
    k = shifted(k_ref, pk_ref, mk_ref)
    v = shifted(v_ref, pv_ref, mv_ref)
    x = shifted(x_ref, px_ref, mx_ref)
    lo = x[:, 0:LANE]
    hi = x[:, LANE:]
    mw = jnp.dot(jnp.tanh(lo).astype(BF16), w2_ref[...], preferred_element_type=F32)
    ma = jnp.dot(lo.astype(BF16), a2_ref[...], preferred_element_type=F32)
    g = jnp.dot(_sigmoid(hi).astype(BF16), g2_ref[...], preferred_element_type=F32)
    log_w = -_softplus(-(w0_ref[...] + mw)) - 0.5
    decay = jnp.exp(-jnp.exp(log_w))
    a = _sigmoid(a0_ref[...] + ma)
    kk = k * kk_ref[...]
    kk = kk / jnp.maximum(jnp.sqrt(_seg_sum(kk * kk, e_ref)), 1e-12)
    k2 = k * (1.0 + (a - 1.0) * ka_ref[...])
    bvec = kk * a
    if masked:
        live = (it * tb + lax.broadcasted_iota(jnp.int32, (tb, 1), 0)) < n_valid
        decay = jnp.where(live, decay, 1.0)
        k2 = jnp.where(live, k2, 0.0)
        bvec = jnp.where(live, bvec, 0.0)
    for ref, val in ((ro_ref, r), (wo_ref, decay), (ko_ref, k2), (zo_ref, -kk), (bo_ref, bvec)):
        for c in range(ref.shape[0]):
            ref[c] = val
    vo_ref[...] = v
    go_ref[...] = g


def _rwkv_prep(proj3, shift, pw, w, xw, n_valid, ncopy):
    b, t, _ = proj3.shape
    tb = _pick(t, TIME_BLOCK_CAP, SUBLANE)
    xcol = (12 * w) // xw
    big = lambda c: pl.BlockSpec((None, tb, w), lambda bb, i: (bb, i, c))
    st = lambda n: pl.BlockSpec((None, 1, n), lambda bb, i: (bb, 0, 0))
    full = lambda shape: pl.BlockSpec(shape, lambda bb, i: (0,) * len(shape))
    one = pl.BlockSpec((None, tb, w), lambda bb, i: (bb, i, 0))
    cop = pl.BlockSpec((ncopy, None, tb, w), lambda bb, i: (0, bb, i, 0))
    one_s = jax.ShapeDtypeStruct((b, t, w), F32)
    cop_s = jax.ShapeDtypeStruct((ncopy, b, t, w), F32)
    sr, sk, sv, sx = shift
    r4, w4, k4, v3, z4, b4, g3 = pl.pallas_call(
        functools.partial(_rwkv_prep_kernel, tb=tb, n_valid=n_valid, masked=n_valid < t),
        out_shape=(cop_s, cop_s, cop_s, one_s, cop_s, cop_s, one_s),
        grid=(b, t // tb),
        in_specs=[big(9), big(10), big(11), pl.BlockSpec((None, tb, xw), lambda bb, i: (bb, i, xcol)),
                  st(w), st(w), st(w), st(xw),
                  full((1, w)), full((1, w)), full((1, w)), full((1, xw)),
                  full((1, w)), full((LANE, w)), full((1, w)), full((LANE, w)), full((xw - LANE, w)),
                  full((1, w)), full((1, w)), full((LANE, LANE))],
        out_specs=(cop, cop, cop, one, cop, cop, one),
        scratch_shapes=[pltpu.VMEM((1, w), F32)] * 3 + [pltpu.VMEM((1, xw), F32)],
        compiler_params=_cp("parallel", "arbitrary"),
        name="rwkv_prep",
    )(proj3, proj3, proj3, proj3, sr, sk, sv, sx, *pw)
    return r4, w4, k4, z4, b4, v3, g3


def _rwkv_scan_kernel(r_ref, w_ref, k_ref, z_ref, b_ref, v_ref, s0_ref, o_ref, so_ref, s_ref, *, tc):
    nk, nv = s_ref.shape[0], s_ref.shape[1]
    vg = min(nv, SCAN_V_GROUP)
    groups = [slice(g0, g0 + vg) for g0 in range(0, nv, vg)]
    kc = min(nk, SCAN_K_CHUNK)
    zero = jnp.zeros((vg, LANE), F32)

    @pl.when(pl.program_id(0) == 0)
    def _():
        s_ref[...] = s0_ref[...]

    def tree(parts):
        parts = list(parts)
        while len(parts) > 1:
            parts = [sum(parts[i:i + 2][1:], parts[i]) for i in range(0, len(parts), 2)]
        return parts[0]

    def step(t, carry):
        def row(ref, kx):
            return ref[t, pl.ds(kx, 1), :]

        def pass1(g, c, acc):
            acc = list(acc)
            for j in range(kc):
                kx = c * kc + j
                acc[j % 4] = acc[j % 4] + s_ref[kx, g, :] * row(z_ref, kx)
            return tuple(acc)

        def pass2(g, sz, vv, c, acc):
            acc = list(acc)
            for j in range(kc):
                kx = c * kc + j
                sn = s_ref[kx, g, :] * row(w_ref, kx) + sz * row(b_ref, kx) + vv * row(k_ref, kx)
                s_ref[kx, g, :] = sn
                acc[j % 4] = acc[j % 4] + sn * row(r_ref, kx)
            return tuple(acc)

        for g in groups:
            sz = tree(lax.fori_loop(0, nk // kc, functools.partial(pass1, g), (zero,) * 4))
            acc = lax.fori_loop(0, nk // kc, functools.partial(pass2, g, sz, v_ref[t, g, :]), (zero,) * 4)
            o_ref[t, g, :] = tree(acc)
        return carry

    lax.fori_loop(0, tc, step, 0)
    so_ref[...] = s_ref[...]


def _rwkv_scan(rl, wl, kl, zl, bl, vl, s0l):
    t, nk, _ = rl.shape
    nv = vl.shape[1]
    tc = _pick(t, SCAN_STEPS_CAP, 1)
    tok = pl.BlockSpec((tc, nk, LANE), lambda i: (i, 0, 0))
    vsp = pl.BlockSpec((tc, nv, LANE), lambda i: (i, 0, 0))
    ssp = pl.BlockSpec((nk, nv, LANE), lambda i: (0, 0, 0))
    return pl.pallas_call(
        functools.partial(_rwkv_scan_kernel, tc=tc),
        out_shape=(jax.ShapeDtypeStruct((t, nv, LANE), F32), jax.ShapeDtypeStruct((nk, nv, LANE), F32)),
        grid=(t // tc,),
        in_specs=[tok, tok, tok, tok, tok, vsp, ssp],
        out_specs=(vsp, ssp),
        scratch_shapes=[pltpu.VMEM((nk, nv, LANE), F32)],
        compiler_params=_cp("arbitrary"),
        name="rwkv_scan",
    )(rl, wl, kl, zl, bl, vl, s0l)


def _rwkv_post_kernel(o_ref, r_ref, k_ref, v_ref, g_ref, lg_ref, lb_ref, rk_ref, e_ref, y_ref):
    o = o_ref[...]
    inv_n = 1.0 / D_HD
    mu = _seg_sum(o, e_ref) * inv_n
    d = o - mu
    var = _seg_sum(d * d, e_ref) * inv_n
    on = d * lax.rsqrt(var + RWKV_GN_EPS) * lg_ref[...] + lb_ref[...]
    bonus = _seg_sum(r_ref[...] * k_ref[...] * rk_ref[...], e_ref) * v_ref[...]
    y_ref[...] = ((on + bonus) * g_ref[...]).astype(y_ref.dtype)


def _rwkv_post(o3, r3, k3, v3, g3, lg, lb, rk, e):
    b, t, w = o3.shape
    tb = _pick(t, TIME_BLOCK_CAP, SUBLANE)
    big = pl.BlockSpec((None, tb, w), lambda bb, i: (bb, i, 0))
    rowsp = pl.BlockSpec((1, w), lambda bb, i: (0, 0))
    return pl.pallas_call(
        _rwkv_post_kernel,
        out_shape=jax.ShapeDtypeStruct((b, t, w), BF16),
        grid=(b, t // tb),
        in_specs=[big] * 5 + [rowsp] * 3 + [pl.BlockSpec((LANE, LANE), lambda bb, i: (0, 0))],
        out_specs=big,
        compiler_params=_cp("parallel", "parallel"),
        name="rwkv_post",
    )(o3, r3, k3, v3, g3, lg, lb, rk, e)


def _rwkv(proj3, shift, s0, pw, w, xw, n_valid):
    b, t, _ = proj3.shape
    nh = w // D_HD
    chains = b * nh
    assert LANE % chains == 0 and D_HD % (LANE // chains) == 0
    vh = LANE // chains
    nv = D_HD // vh
    prep_w, (lg, lb, rk, e) = pw
    r4, w4, k4, z4, b4, v3, g3 = _rwkv_prep(proj3, shift, prep_w, w, xw, n_valid, vh)
    r3, k3 = r4[0], k4[0]

    def klay(x4):
        return x4.reshape(vh, b, t, nh, D_HD).transpose(2, 4, 0, 1, 3).reshape(t, D_HD, LANE)

    def vlay(x3):
        return x3.reshape(b, t, nh, vh, nv).transpose(1, 4, 3, 0, 2).reshape(t, nv, LANE)

    s0l = s0.reshape(b, nh, vh, nv, D_HD).transpose(4, 3, 2, 0, 1).reshape(D_HD, nv, LANE)
    ol, sl = _rwkv_scan(klay(r4), klay(w4), klay(k4), klay(z4), klay(b4), vlay(v3), s0l)
    o3 = ol.reshape(t, nv, vh, b, nh).transpose(3, 0, 4, 2, 1).reshape(b, t, w)
    s_new = sl.reshape(D_HD, nv, vh, b, nh).transpose(3, 4, 2, 1, 0).reshape(b, nh, D_HD, D_HD)
    y = _rwkv_post(o3, r3, k3, v3, g3, lg, lb, rk, e)
    return y, s_new


def _layer(x3, n_valid, tables, state, lbv, p, dims):
    d, w, xw, n_in = dims
    b, t, _ = x3.shape
    k_past, v_past, s_a, h_b, buf_b, s_d, shift_d, buf_f = state
    m = b * t
    x2 = x3.reshape(m, d)
    h = _rmsnorm_bf16(x2, p['ln_mix_g'])
    proj = _matmul(h, p['w_in'], tm_cap=MM_TM_CAP, tn_cap=1152, tk_cap=d, w_resident=True)
    proj3 = proj.reshape(b, t, -1)
    y_a, s_a_t = _hgrn(proj3, lbv, p['hgrn_norm_g'].astype(F32), jnp.swapaxes(s_a, -1, -2), w, n_valid)
    s_a_new = jnp.swapaxes(s_a_t, -1, -2)
    y_b, h_b_new, buf_b_new = _lru(proj3, buf_b, h_b, p, w, n_valid)
    qn, kn, v_new, kb, vb = _qk_prep(proj3, p['attn_q_norm_g'], p['attn_k_norm_g'], tables, w)
    if k_past is None:
        y_c = _attn_prompt(qn, kb, vb, w)
    else:
        (k_all, layer), (v_all, _) = k_past, v_past
        y_c = _attn_sample(qn, k_all, v_all, layer, kb, vb, w, n_valid)
    sh = shift_d.reshape(b, 1, -1)
    shift = (sh[:, :, 0:w], sh[:, :, w:2 * w], sh[:, :, 2 * w:3 * w],
             jnp.pad(sh[:, :, 3 * w:], ((0, 0), (0, 0), (0, xw - (sh.shape[2] - 3 * w)))))
    y_d, s_d_new = _rwkv(proj3, shift, s_d, p['rwkv'], w, xw, n_valid)
    shift_new = proj3[:, n_valid - 1, 9 * w:n_in]
    y_mix = jnp.concatenate([y_a, y_b, y_c, y_d], axis=-1).reshape(m, 4 * w)
    x2 = _matmul(y_mix, p['w_out'], x2, tm_cap=MM_TM_CAP, tn_cap=512, tk_cap=4 * w, w_resident=True)
    h = _rmsnorm_bf16(x2, p['ln_ffn_g'])
    z, buf_f_new = _ffn_gate_up(h.reshape(b, t, d), p['ffn_w_gate'], p['ffn_w_up'], p['ffn_conv_w'], buf_f, n_valid)
    f = z.shape[-1]
    x2 = _matmul(z.reshape(m, f), p['ffn_w_down'], x2, tm_cap=MM_TM_CAP, tn_cap=512, tk_cap=f // 2,
                 w_resident=False)
    keep = slice(max(n_valid - C_WIN, 0), n_valid) if k_past is None else slice(0, n_valid)
    new_state = (kn[:, keep], v_new[:, keep], s_a_new, h_b_new.reshape(b, w), buf_b_new,
                 s_d_new, shift_new, buf_f_new)
    return x2.reshape(b, t, d), new_state


def kernel(x_prompt, x_sample, cache_attn_k, cache_attn_v, state_hgrn, state_lru_h, state_lru_conv, state_rwkv, state_rwkv_shift, state_ffn_conv, ln_mix_g, w_in, hgrn_lb_logits, hgrn_norm_g, lru_conv_w, lru_conv_b, lru_wa, lru_ba, lru_wx, lru_bx, lru_lambda, attn_q_norm_g, attn_k_norm_g, rwkv_mu, rwkv_w0, rwkv_w2, rwkv_a0, rwkv_a2, rwkv_g2, rwkv_k_k, rwkv_k_a, rwkv_r_k, rwkv_ln_g, rwkv_ln_b, w_out, ln_ffn_g, ffn_w_gate, ffn_w_up, ffn_conv_w, ffn_w_down):
    bp, tp, d = x_prompt.shape
    bs, ts, _ = x_sample.shape
    depth = w_in.shape[0]
    n_in = w_in.shape[2]
    w = d // 4
    n_pad = -(-n_in // LANE) * LANE
    rank_w, rank_a, rank_g = rwkv_w2.shape[1], rwkv_a2.shape[1], rwkv_g2.shape[1]
    assert rank_w + rank_a == LANE
    xw = n_pad - 12 * w
    dims = (d, w, xw, n_in)
    row = lambda v: v.reshape(1, -1).astype(F32)

    lb_soft = jax.nn.softmax(hgrn_lb_logits.astype(F32), axis=0)
    lb_all = jnp.cumsum(lb_soft, axis=0) - lb_soft[:1]

    to_bf16 = lambda x: x.astype(BF16)
    zeros = lambda r: jnp.zeros((r, w), F32)
    head_id = np.arange(LANE) // D_HD
    e_blk = jnp.asarray((head_id[:, None] == head_id[None, :]).astype(np.float32))

    layers = []
    for l in range(depth):
        mu = rwkv_mu[l].astype(F32)
        prep_w = (
            row(mu[0:w]), row(mu[w:2 * w]), row(mu[2 * w:3 * w]),
            row(jnp.pad(mu[3 * w:], (0, xw - (mu.shape[0] - 3 * w)))),
            row(rwkv_w0[l]),
            jnp.concatenate([rwkv_w2[l].astype(F32), zeros(rank_a)], axis=0).astype(BF16),
            row(rwkv_a0[l]),
            jnp.concatenate([zeros(rank_w), rwkv_a2[l].astype(F32)], axis=0).astype(BF16),
            jnp.concatenate([rwkv_g2[l].astype(F32), zeros(xw - LANE - rank_g)], axis=0).astype(BF16),
            row(rwkv_k_k[l]), row(rwkv_k_a[l]), e_blk,
        )
        post_w = (row(rwkv_ln_g[l]), row(rwkv_ln_b[l]), row(rwkv_r_k[l]), e_blk)
        layers.append({
            'ln_mix_g': ln_mix_g[l], 'w_in': to_bf16(jnp.pad(w_in[l], ((0, 0), (0, n_pad - n_in)))),
            'hgrn_norm_g': hgrn_norm_g[l],
            'lru_conv_w': lru_conv_w[l], 'lru_conv_b': lru_conv_b[l], 'lru_wa': lru_wa[l],
            'lru_ba': lru_ba[l], 'lru_wx': lru_wx[l], 'lru_bx': lru_bx[l], 'lru_lambda': lru_lambda[l],
            'attn_q_norm_g': attn_q_norm_g[l], 'attn_k_norm_g': attn_k_norm_g[l],
            'rwkv': (prep_w, post_w), 'w_out': to_bf16(w_out[l]), 'ln_ffn_g': ln_ffn_g[l],
            'ffn_w_gate': to_bf16(ffn_w_gate[l]), 'ffn_w_up': to_bf16(ffn_w_up[l]),
            'ffn_conv_w': ffn_conv_w[l].astype(F32), 'ffn_w_down': to_bf16(ffn_w_down[l]),
        })

    f = ffn_w_gate.shape[2]
    nh_a, nh_d = w // A_HD, w // D_HD
    zero_state = (None, None, jnp.zeros((bp, nh_a, A_HD, A_HD), F32), jnp.zeros((bp, w), F32),
                  jnp.zeros((bp, 3, w), F32), jnp.zeros((bp, nh_d, D_HD, D_HD), F32),
                  jnp.zeros((bp, n_in - 9 * w), F32), jnp.zeros((bp, 2, f), F32))
    tab_p = _rope_tables(jnp.arange(tp, dtype=jnp.int32))
    tab_s = _rope_tables(PAST_LEN + jnp.arange(PAD_T, dtype=jnp.int32))
    lp = cache_attn_k.shape[2]
    cache_k = cache_attn_k.reshape(depth, bs, lp, w)
    cache_v = cache_attn_v.reshape(depth, bs, lp, w)
    xp = x_prompt.astype(F32)
    xs = jnp.pad(x_sample.astype(F32), ((0, 0), (0, PAD_T - ts), (0, 0)))
    p_states, s_states = [], []
    for l in range(depth):
        xp, st_p = _layer(xp, tp, tab_p, zero_state, lb_all[l], layers[l], dims)
        p_states.append(st_p)
        s_in = ((cache_k, l), (cache_v, l), state_hgrn[l].astype(F32),
                state_lru_h[l].astype(F32), state_lru_conv[l].astype(F32), state_rwkv[l].astype(F32),
                state_rwkv_shift[l].astype(F32), state_ffn_conv[l].astype(F32))
        xs, st_s = _layer(xs, ts, tab_s, s_in, lb_all[l], layers[l], dims)
        s_states.append(st_s)

    dp, ds = x_prompt.dtype, x_sample.dtype
    nh_c = w // C_HD

    def stack(states, i, dt, shape=None):
        out = jnp.stack([st[i] for st in states]).astype(dt)
        return out if shape is None else out.reshape((depth,) + shape)

    outs = [xp.astype(dp), xs[:, :ts].astype(ds)]
    for states, bb, tt, dt in ((p_states, bp, min(C_WIN, tp), dp), (s_states, bs, ts, ds)):
        outs += [stack(states, 0, dt, (bb, tt, nh_c, C_HD)), stack(states, 1, dt, (bb, tt, nh_c, C_HD)),
                 stack(states, 2, dt), stack(states, 3, dt), stack(states, 4, dt), stack(states, 5, dt),
                 stack(states, 6, dt), stack(states, 7, dt)]
    return tuple(outs)
```

```python
import functools
import math

import numpy as np
import jax
import jax.numpy as jnp
from jax import lax
from jax.experimental import pallas as pl
from jax.experimental.pallas import tpu as pltpu

F32 = jnp.float32
BF16 = jnp.bfloat16

PAST_LEN = 8192
A_HD = 128
B_HD = 128
C_HD = 128
D_HD = 64
ROPE_DIM = C_HD // 4
ROPE_THETA = 500000.0
C_PATTERNS = ((128, 1), (512, 4), (2048, 16))
LRU_C = 8.0
RMS_EPS = 1e-6
RWKV_GN_EPS = 64e-5
NEG_BIG = -30000.0
F_FLOOR = 1e-20

LANE = 128
SUBLANE = 8
VMEM_LIMIT_BYTES = 56 * 2**20

C_WIN = max(win for win, _ in C_PATTERNS)

TIME_BLOCK_CAP = 256
MM_TM_CAP = 1024
FFN_TN = 256
FFN_ROW_PIECES = 1
SCAN_STEPS_CAP = 32
SCAN_V_GROUP = 32
SCAN_K_CHUNK = 32

HGRN_CHUNK = 16
HGRN_ROW_GROUP = 64
PAD_T = 16
ATT_BLK = 256
MASK_BIAS = -1e30
HI = lax.Precision.HIGHEST


def _cp(*sem):
    return pltpu.CompilerParams(dimension_semantics=sem, vmem_limit_bytes=VMEM_LIMIT_BYTES)


def _pick(n, cap, quantum):
    if n <= cap:
        return n
    t = (cap // quantum) * quantum
    while t >= quantum:
        if n % t == 0:
            return t
        t -= quantum
    raise ValueError(f"no tile for {n} (cap {cap}, quantum {quantum})")


def _sigmoid(x):
    return jax.nn.sigmoid(x)


def _silu(x):
    return x * jax.nn.sigmoid(x)


def _softplus(x):
    return jnp.maximum(x, 0.0) + jnp.log(1.0 + jnp.exp(-jnp.abs(x)))


def _gelu_tanh(x):
    return x * (0.5 * (1.0 + jnp.tanh(math.sqrt(2.0 / math.pi) * (x + 0.044715 * (x * x * x)))))


def _norm_kernel(x_ref, g_ref, o_ref):
    x = x_ref[...]
    inv = lax.rsqrt(jnp.mean(x * x, axis=-1, keepdims=True) + RMS_EPS)
    o_ref[...] = (x * inv * g_ref[...]).astype(o_ref.dtype)


def _rmsnorm_bf16(x2, g):
    m, d = x2.shape
    tm = _pick(m, 256, SUBLANE)
    return pl.pallas_call(
        _norm_kernel,
        out_shape=jax.ShapeDtypeStruct((m, d), BF16),
        grid=(m // tm,),
        in_specs=[pl.BlockSpec((tm, d), lambda i: (i, 0)), pl.BlockSpec((1, d), lambda i: (0, 0))],
        out_specs=pl.BlockSpec((tm, d), lambda i: (i, 0)),
        compiler_params=_cp("parallel"),
        name="rmsnorm",
    )(x2, g.reshape(1, d).astype(F32))


def _mm_kernel(*refs, nk, has_res):
    if has_res:
        a_ref, w_ref, r_ref, o_ref = refs[:4]
    else:
        a_ref, w_ref, o_ref = refs[:3]
        r_ref = None
    part = jnp.dot(a_ref[...], w_ref[...], preferred_element_type=F32)
    if nk == 1:
        o_ref[...] = (r_ref[...] + part) if has_res else part
        return
    acc_ref = refs[-1]
    k = pl.program_id(2)

    @pl.when(k == 0)
    def _():
        acc_ref[...] = part

    @pl.when(k > 0)
    def _():
        acc_ref[...] += part

    @pl.when(k == nk - 1)
    def _():
        o_ref[...] = (r_ref[...] + acc_ref[...]) if has_res else acc_ref[...]


def _matmul(a, w, layer, res=None, *, tm_cap, tn_cap, tk_cap, w_resident):
    m, k = a.shape
    n = w.shape[2]
    tm = _pick(m, tm_cap, SUBLANE)
    tn = _pick(n, tn_cap, LANE)
    tk = _pick(k, tk_cap, LANE)
    nk = k // tk
    if w_resident:
        grid = (n // tn, m // tm, nk)
        amap = lambda j, i, kk: (i, kk)
        wmap = lambda j, i, kk: (layer, kk, j)
        omap = lambda j, i, kk: (i, j)
    else:
        grid = (m // tm, n // tn, nk)
        amap = lambda i, j, kk: (i, kk)
        wmap = lambda i, j, kk: (layer, kk, j)
        omap = lambda i, j, kk: (i, j)
    in_specs = [pl.BlockSpec((tm, tk), amap), pl.BlockSpec((None, tk, tn), wmap)]
    args = [a, w]
    if res is not None:
        in_specs.append(pl.BlockSpec((tm, tn), omap))
        args.append(res)
    scratch = [pltpu.VMEM((tm, tn), F32)] if nk > 1 else []
    return pl.pallas_call(
        functools.partial(_mm_kernel, nk=nk, has_res=res is not None),
        out_shape=jax.ShapeDtypeStruct((m, n), F32),
        grid=grid,
        in_specs=in_specs,
        out_specs=pl.BlockSpec((tm, tn), omap),
        scratch_shapes=scratch,
        compiler_params=_cp("parallel", "parallel", "arbitrary"),
        name="matmul_res" if res is not None else "matmul",
    )(*args)


def _ffn_kernel(h_ref, wg_ref, wu_ref, cw_ref, st_ref, z_ref, so_ref, gs_ref, *, nseq, ts, nsub, i_state,
                r_state):
    i = pl.program_id(2)

    @pl.when(i == 0)
    def _():
        for s in range(nseq):
            gs_ref[s, 6:8, :] = st_ref[s]

    @pl.when(i > 0)
    def _():
        for s in range(nseq):
            gs_ref[s, 0:8, :] = gs_ref[s, ts:ts + 8, :]

    cw = cw_ref[...]
    piece = ts // nsub
    for s in range(nseq):
        for r in range(nsub):
            lo = r * piece
            rows = slice(s * ts + lo, s * ts + lo + piece)
            h = h_ref[rows, :]
            g = jnp.dot(h, wg_ref[...], preferred_element_type=F32)
            up = jnp.dot(h, wu_ref[...], preferred_element_type=F32)
            gs_ref[s, 8 + lo:8 + lo + piece, :] = g
            y = gs_ref[s, 6 + lo:6 + lo + piece, :] * cw[0:1, :]
            y = y + gs_ref[s, 7 + lo:7 + lo + piece, :] * cw[1:2, :]
            y = y + g * cw[2:3, :]
            z_ref[rows, :] = (_silu(y) * up).astype(z_ref.dtype)

    @pl.when(i == i_state)
    def _():
        for s in range(nseq):
            so_ref[s] = gs_ref[s, 8 + r_state:10 + r_state, :]


def _ffn_gate_up(h3, wg, wu, layer, cw, st, n_valid):
    b, t, d = h3.shape
    f = wg.shape[2]
    tn = _pick(f, FFN_TN, LANE)
    if b * t <= MM_TM_CAP:
        nseq, ts, nb, nt = b, t, 1, 1
    else:
        nseq, ts, nb, nt = 1, _pick(t, MM_TM_CAP, SUBLANE), b, t // _pick(t, MM_TM_CAP, SUBLANE)
    tm = nseq * ts
    i_state, r_state = divmod(n_valid - 2, ts)
    assert r_state + 2 <= ts
    z, so = pl.pallas_call(
        functools.partial(_ffn_kernel, nseq=nseq, ts=ts, nsub=FFN_ROW_PIECES if nseq == 1 and ts % 32 == 0 else 1,
                          i_state=i_state, r_state=r_state),
        out_shape=(jax.ShapeDtypeStruct((nb, nt * tm, f), BF16), jax.ShapeDtypeStruct((b, 2, f), F32)),
        grid=(f // tn, nb, nt),
        in_specs=[
            pl.BlockSpec((None, tm, d), lambda j, bb, i: (bb, i, 0)),
            pl.BlockSpec((None, d, tn), lambda j, bb, i: (layer, 0, j)),
            pl.BlockSpec((None, d, tn), lambda j, bb, i: (layer, 0, j)),
            pl.BlockSpec((3, tn), lambda j, bb, i: (0, j)),
            pl.BlockSpec((nseq, 2, tn), lambda j, bb, i: (bb, 0, j)),
        ],
        out_specs=(
            pl.BlockSpec((None, tm, tn), lambda j, bb, i: (bb, i, j)),
            pl.BlockSpec((nseq, 2, tn), lambda j, bb, i: (bb, 0, j)),
        ),
        scratch_shapes=[pltpu.VMEM((nseq, ts + 8, tn), F32)],
        compiler_params=_cp("parallel", "parallel", "arbitrary"),
        name="ffn_gate_up",
    )(h3.reshape(nb, nt * tm, d), wg, wu, cw, st)
    return z.reshape(b, t, f), so


def _hgrn_kernel(q_ref, f_ref, i_ref, g_ref, lb_ref, ng_ref, tri_ref, s0_ref, y_ref, so_ref,
                 st_ref, kp_ref, gp_ref, vp_ref, o_ref, qs_ref, ud_ref, *, tb, n_valid, masked, n_chunks):
    it = pl.program_id(2)
    ch = HGRN_CHUNK

    @pl.when(it == 0)
    def _():
        st_ref[...] = s0_ref[...]

    lb = lb_ref[...]
    fl = f_ref[...]
    f = lb + (1.0 - lb) * _sigmoid(fl)
    lf = jnp.log(jnp.maximum(f, F_FLOOR))
    kk = (1.0 - lb) * _sigmoid(-fl)
    row = lax.broadcasted_iota(jnp.int32, (tb, 1), 0)
    if masked:
        live = (it * tb + row) < n_valid
        lf = jnp.where(live, lf, 0.0)
        kk = jnp.where(live, kk, 0.0)
    q = _silu(q_ref[...]) * (A_HD ** -0.5)
    v = i_ref[...]
    g = jnp.dot(tri_ref[...], lf, precision=HI, preferred_element_type=F32)
    zero = jnp.zeros((ch, A_HD), F32)
    for ref, val in ((kp_ref, kk), (gp_ref, g), (vp_ref, v)):
        ref[0:ch, :] = zero
        ref[ch:tb + ch, :] = val
    qs_ref[...] = q
    qe = (q * jnp.exp(g)).astype(BF16)

    for c in range(n_chunks):
        lo = c * ch
        gl = gp_ref[lo + 2 * ch - 1:lo + 2 * ch, :]
        kt = kk[lo:lo + ch, :] * jnp.exp(gl - g[lo:lo + ch, :])
        ud_ref[c] = lax.dot_general(v[lo:lo + ch, :].astype(BF16), kt.astype(BF16),
                                    (((0,), (0,)), ((), ())), preferred_element_type=F32)
    st = st_ref[...]
    for c in range(n_chunks):
        lo = c * ch
        o_ref[lo:lo + ch, :] = lax.dot_general(qe[lo:lo + ch, :], st.astype(BF16),
                                               (((1,), (1,)), ((), ())), preferred_element_type=F32)
        st = st * jnp.exp(gp_ref[lo + 2 * ch - 1:lo + 2 * ch, :]) + ud_ref[c]
    st_ref[...] = st
    so_ref[...] = st

    grp = min(tb, HGRN_ROW_GROUP)
    pos = lax.broadcasted_iota(jnp.int32, (grp, 1), 0) & (ch - 1)
    for r0 in range(0, tb, grp):
        qg = qs_ref[r0:r0 + grp, :]
        gg = gp_ref[ch + r0:ch + r0 + grp, :]
        o = o_ref[r0:r0 + grp, :]
        for d in range(ch):
            lo = ch + r0 - d
            dec = jnp.exp(jnp.where(pos >= d, gg - gp_ref[lo:lo + grp, :], NEG_BIG))
            a = jnp.sum(qg * kp_ref[lo:lo + grp, :] * dec, axis=-1, keepdims=True)
            o = o + a * vp_ref[lo:lo + grp, :]
        o = o * lax.rsqrt(jnp.mean(o * o, axis=-1, keepdims=True) + RMS_EPS) * ng_ref[...]
        y_ref[r0:r0 + grp, :] = (o * _silu(g_ref[r0:r0 + grp, :])).astype(y_ref.dtype)


def _hgrn(proj3, lb, norm_g, s0_t, w, n_valid):
    b, t, _ = proj3.shape
    nh = w // A_HD
    tb = _pick(t, TIME_BLOCK_CAP, HGRN_CHUNK)
    ch = HGRN_CHUNK
    assert ch & (ch - 1) == 0
    idx = np.arange(tb)
    tri = ((idx[:, None] // ch == idx[None, :] // ch) & (idx[None, :] <= idx[:, None])).astype(np.float32)

    def col(off):
        return lambda bb, h, i: (bb, i, off * nh + h)

    return pl.pallas_call(
        functools.partial(_hgrn_kernel, tb=tb, n_valid=n_valid, masked=n_valid < t, n_chunks=tb // ch),
        out_shape=(jax.ShapeDtypeStruct((b, t, w), BF16), jax.ShapeDtypeStruct((b, nh, A_HD, A_HD), F32)),
        grid=(b, nh, t // tb),
        in_specs=[
            pl.BlockSpec((None, tb, A_HD), col(0)),
            pl.BlockSpec((None, tb, A_HD), col(1)),
            pl.BlockSpec((None, tb, A_HD), col(2)),
            pl.BlockSpec((None, tb, A_HD), col(3)),
            pl.BlockSpec((1, A_HD), lambda bb, h, i: (0, h)),
            pl.BlockSpec((1, A_HD), lambda bb, h, i: (0, 0)),
            pl.BlockSpec((tb, tb), lambda bb, h, i: (0, 0)),
            pl.BlockSpec((None, None, A_HD, A_HD), lambda bb, h, i: (bb, h, 0, 0)),
        ],
        out_specs=(
            pl.BlockSpec((None, tb, A_HD), lambda bb, h, i: (bb, i, h)),
            pl.BlockSpec((None, None, A_HD, A_HD), lambda bb, h, i: (bb, h, 0, 0)),
        ),
        scratch_shapes=[pltpu.VMEM((A_HD, A_HD), F32)] + [pltpu.VMEM((tb + ch, A_HD), F32)] * 3
        + [pltpu.VMEM((tb, A_HD), F32)] * 2 + [pltpu.VMEM((tb // ch, A_HD, A_HD), F32)],
        compiler_params=_cp("parallel", "parallel", "arbitrary"),
        name="hgrn2",
    )(proj3, proj3, proj3, proj3, lb.reshape(1, w), norm_g.reshape(1, A_HD), jnp.asarray(tri), s0_t)


def _lru_kernel(x_ref, g_ref, buf_ref, h0_ref, cw_ref, cb_ref, wa_ref, ba_ref, wx_ref, bx_ref, lam_ref,
                y_ref, hl_ref, bo_ref, xs_ref, a_ref, b_ref, hc_ref, *, tb, n_valid, masked, i_state, r_state):
    it = pl.program_id(1)
    nh = wa_ref.shape[0]

    @pl.when(it == 0)
    def _():
        xs_ref[5:8, :] = buf_ref[...]
        hc_ref[...] = h0_ref[...]

    @pl.when(it > 0)
    def _():
        xs_ref[0:8, :] = xs_ref[tb:tb + 8, :]

    x = x_ref[...]
    xs_ref[8:tb + 8, :] = x
    cw = cw_ref[...]
    xc = xs_ref[5:tb + 5, :] * cw[0:1, :]
    xc = xc + xs_ref[6:tb + 6, :] * cw[1:2, :]
    xc = xc + xs_ref[7:tb + 7, :] * cw[2:3, :]
    xc = xc + x * cw[3:4, :]
    xc = xc + cb_ref[...]
    xcb = xc.astype(BF16)
    ra, rx = [], []
    for h in range(nh):
        xh = xcb[:, h * B_HD:(h + 1) * B_HD]
        ra.append(jnp.dot(xh, wa_ref[h], preferred_element_type=F32))
        rx.append(jnp.dot(xh, wx_ref[h], preferred_element_type=F32))
    r = _sigmoid(jnp.concatenate(ra, axis=1) + ba_ref[...])
    ig = _sigmoid(jnp.concatenate(rx, axis=1) + bx_ref[...])
    log_a = (-LRU_C) * r * _softplus(-lam_ref[...])
    a = jnp.exp(log_a)
    om = -jnp.tanh(log_a) * (a * a + 1.0)
    bb = jnp.sqrt(jnp.maximum(om, 0.0)) * (ig * xc)
    if masked:
        row = it * tb + lax.broadcasted_iota(jnp.int32, (tb, 1), 0)
        live = row < n_valid
        a = jnp.where(live, a, 1.0)
        bb = jnp.where(live, bb, 0.0)
    a_ref[...] = a
    b_ref[...] = bb

    def step(t, h):
        h = a_ref[pl.ds(t, 1), :] * h + b_ref[pl.ds(t, 1), :]
        b_ref[pl.ds(t, 1), :] = h
        return h

    h = lax.fori_loop(0, tb, step, hc_ref[...], unroll=8)
    hc_ref[...] = h
    hl_ref[...] = h
    y_ref[...] = (b_ref[...] * _gelu_tanh(g_ref[...])).astype(y_ref.dtype)

    @pl.when(it == i_state)
    def _():
        bo_ref[...] = xs_ref[8 + r_state:11 + r_state, :]


def _lru(proj3, buf, h0, p, w, n_valid):
    b, t, _ = proj3.shape
    nh = w // B_HD
    tb = _pick(t, TIME_BLOCK_CAP, SUBLANE)
    i_state, r_state = divmod(n_valid - 3, tb)
    assert n_valid >= 3 and r_state + 3 <= tb
    row = lambda v: v.reshape(1, w).astype(F32)
    full = lambda shape: pl.BlockSpec(shape, lambda bb, i: (0,) * len(shape))
    return pl.pallas_call(
        functools.partial(_lru_kernel, tb=tb, n_valid=n_valid, masked=n_valid < t,
                          i_state=i_state, r_state=r_state),
        out_shape=(jax.ShapeDtypeStruct((b, t, w), BF16), jax.ShapeDtypeStruct((b, 1, w), F32),
                   jax.ShapeDtypeStruct((b, 3, w), F32)),
        grid=(b, t // tb),
        in_specs=[
            pl.BlockSpec((None, tb, w), lambda bb, i: (bb, i, 4)),
            pl.BlockSpec((None, tb, w), lambda bb, i: (bb, i, 5)),
            pl.BlockSpec((None, 3, w), lambda bb, i: (bb, 0, 0)),
            pl.BlockSpec((None, 1, w), lambda bb, i: (bb, 0, 0)),
            full((4, w)), full((1, w)), full((nh, B_HD, B_HD)), full((1, w)),
            full((nh, B_HD, B_HD)), full((1, w)), full((1, w)),
        ],
        out_specs=(
            pl.BlockSpec((None, tb, w), lambda bb, i: (bb, i, 0)),
            pl.BlockSpec((None, 1, w), lambda bb, i: (bb, 0, 0)),
            pl.BlockSpec((None, 3, w), lambda bb, i: (bb, 0, 0)),
        ),
        scratch_shapes=[pltpu.VMEM((tb + 8, w), F32), pltpu.VMEM((tb, w), F32), pltpu.VMEM((tb, w), F32),
                        pltpu.VMEM((1, w), F32)],
        compiler_params=_cp("parallel", "arbitrary"),
        name="rglru",
    )(proj3, proj3, buf, h0.reshape(b, 1, w), p['lru_conv_w'].astype(F32), row(p['lru_conv_b']),
      p['lru_wa'].astype(BF16), row(p['lru_ba']), p['lru_wx'].astype(BF16), row(p['lru_bx']),
      row(p['lru_lambda']))


def _qk_kernel(q_ref, k_ref, v_ref, qg_ref, kg_ref, c_ref, sa_ref, sb_ref,
               qo_ref, ko_ref, vo_ref, kb_ref, vb_ref):
    nh = q_ref.shape[1] // C_HD
    cosf, sina, sinb = c_ref[...], sa_ref[...], sb_ref[...]

    def prep(x, g):
        xn = x * lax.rsqrt(jnp.mean(x * x, axis=-1, keepdims=True) + RMS_EPS) * g
        up = pltpu.roll(xn, C_HD - ROPE_DIM // 2, axis=1)
        dn = pltpu.roll(xn, ROPE_DIM // 2, axis=1)
        return xn * cosf + up * sina + dn * sinb

    for h in range(nh):
        sl = slice(h * C_HD, (h + 1) * C_HD)
        qo_ref[:, sl] = (prep(q_ref[:, sl], qg_ref[...]) * (C_HD ** -0.5)).astype(qo_ref.dtype)
        kn = prep(k_ref[:, sl], kg_ref[...])
        ko_ref[:, sl] = kn
        kb_ref[:, sl] = kn.astype(kb_ref.dtype)
    v = v_ref[...]
    vo_ref[...] = v
    vb_ref[...] = v.astype(vb_ref.dtype)


def _rope_tables(pos):
    half = ROPE_DIM // 2
    inv = ROPE_THETA ** (-jnp.arange(half, dtype=F32) / half)
    ang = pos.astype(F32)[:, None] * inv[None, :]
    cos, sin = jnp.cos(ang), jnp.sin(ang)
    n = pos.shape[0]
    rest = C_HD - ROPE_DIM
    cosf = jnp.concatenate([cos, cos, jnp.ones((n, rest), F32)], axis=1)
    sina = jnp.concatenate([-sin, jnp.zeros((n, half + rest), F32)], axis=1)
    sinb = jnp.concatenate([jnp.zeros((n, half), F32), sin, jnp.zeros((n, rest), F32)], axis=1)
    return cosf, sina, sinb


def _qk_prep(proj3, qg, kg, tables, w):
    b, t, _ = proj3.shape
    tb = _pick(t, TIME_BLOCK_CAP, SUBLANE)
    tab = pl.BlockSpec((tb, C_HD), lambda bb, i: (i, 0))
    gsp = pl.BlockSpec((1, C_HD), lambda bb, i: (0, 0))
    src = lambda c: pl.BlockSpec((None, tb, w), lambda bb, i: (bb, i, c))
    dst = pl.BlockSpec((None, tb, w), lambda bb, i: (bb, i, 0))
    return pl.pallas_call(
        _qk_kernel,
        out_shape=(jax.ShapeDtypeStruct((b, t, w), BF16), jax.ShapeDtypeStruct((b, t, w), F32),
                   jax.ShapeDtypeStruct((b, t, w), F32),
                   jax.ShapeDtypeStruct((b, t, w), BF16), jax.ShapeDtypeStruct((b, t, w), BF16)),
        grid=(b, t // tb),
        in_specs=[src(6), src(7), src(8), gsp, gsp, tab, tab, tab],
        out_specs=(dst, dst, dst, dst, dst),
        compiler_params=_cp("parallel", "parallel"),
        name="qk_norm_rope",
    )(proj3, proj3, proj3, qg.reshape(1, C_HD).astype(F32), kg.reshape(1, C_HD).astype(F32), *tables)


def _pattern_bias(dist):
    dist = np.asarray(dist, np.int64)
    cnt = np.zeros(dist.shape, np.int64)
    for win, dil in C_PATTERNS:
        cnt += (dist >= 0) & (dist <= win) & (dist % dil == 0)
    return np.where(cnt > 0, np.log(np.maximum(cnt, 1)), MASK_BIAS).astype(np.float32)


def _attn_kernel(q_ref, k_ref, v_ref, bias_ref, o_ref, *, n_back, span):
    qi = pl.program_id(2)
    rows = pl.ds(pl.multiple_of(jnp.maximum(qi - n_back, 0) * ATT_BLK, ATT_BLK), span)
    s = lax.dot_general(q_ref[...], k_ref[rows, :], (((1,), (1,)), ((), ())), preferred_element_type=F32)
    s = s + bias_ref[...]
    m = jnp.max(s, axis=-1, keepdims=True)
    p = jnp.exp(s - m)
    l = jnp.sum(p, axis=-1, keepdims=True)
    acc = jnp.dot(p.astype(BF16), v_ref[rows, :], preferred_element_type=F32)
    o_ref[...] = (acc / l).astype(o_ref.dtype)


def _attn_prompt(qn, kb, vb, w):
    b, t, _ = qn.shape
    nh = w // C_HD
    blk = ATT_BLK
    assert t % blk == 0
    n_back = -(-C_WIN // blk)
    n_span = min(n_back + 1, t // blk)
    r = np.arange(blk)
    n_tab = min(n_back, n_span - 1) + 1
    wide = jnp.asarray(_pattern_bias((n_tab - 1) * blk + r[:, None] - np.arange((n_tab - 1 + n_span) * blk)[None, :]))
    bias = jnp.stack([wide[:, (n_tab - 1 - k) * blk:(n_tab - 1 - k + n_span) * blk] for k in range(n_tab)])
    return pl.pallas_call(
        functools.partial(_attn_kernel, n_back=n_back, span=n_span * blk),
        out_shape=jax.ShapeDtypeStruct((b, t, w), BF16),
        grid=(b, nh, t // blk),
        in_specs=[
            pl.BlockSpec((None, blk, C_HD), lambda bb, h, i: (bb, i, h)),
            pl.BlockSpec((None, t, C_HD), lambda bb, h, i: (bb, 0, h)),
            pl.BlockSpec((None, t, C_HD), lambda bb, h, i: (bb, 0, h)),
            pl.BlockSpec((None, blk, n_span * blk), lambda bb, h, i: (jnp.minimum(i, n_tab - 1), 0, 0)),
        ],
        out_specs=pl.BlockSpec((None, blk, C_HD), lambda bb, h, i: (bb, i, h)),
        compiler_params=_cp("parallel", "parallel", "arbitrary"),
        name="attn_prompt",
    )(qn, kb, vb, bias)


def _attn_s_kernel(q_ref, kp_ref, vp_ref, kn_ref, vn_ref, bp_ref, bn_ref, o_ref):
    q = q_ref[...]
    nt = (((1,), (1,)), ((), ()))
    sp = lax.dot_general(q, kp_ref[...].astype(BF16), nt, preferred_element_type=F32) + bp_ref[...]
    sn = lax.dot_general(q, kn_ref[...], nt, preferred_element_type=F32) + bn_ref[...]
    m = jnp.maximum(jnp.max(sp, axis=-1, keepdims=True), jnp.max(sn, axis=-1, keepdims=True))
    pp = jnp.exp(sp - m)
    pn = jnp.exp(sn - m)
    l = jnp.sum(pp, axis=-1, keepdims=True) + jnp.sum(pn, axis=-1, keepdims=True)
    acc = jnp.dot(pp.astype(BF16), vp_ref[...].astype(BF16), preferred_element_type=F32)
    acc = acc + jnp.dot(pn.astype(BF16), vn_ref[...], preferred_element_type=F32)
    o_ref[...] = (acc / l).astype(o_ref.dtype)


def _attn_sample(qn, kpast, vpast, layer, kb, vb, w, n_valid):
    b, tp, _ = qn.shape
    lp = kpast.shape[2]
    nh = w // C_HD
    t = np.arange(tp)
    live = t < n_valid
    bp = jnp.asarray(np.where(live[:, None], _pattern_bias(lp + t[:, None] - np.arange(lp)[None, :]), 0.0),
                     F32)
    bn = jnp.asarray(np.where(live[:, None] & live[None, :], _pattern_bias(t[:, None] - t[None, :]),
                              MASK_BIAS), F32)
    new = pl.BlockSpec((None, tp, C_HD), lambda bb, h: (bb, 0, h))
    past = pl.BlockSpec((None, None, lp, C_HD), lambda bb, h: (layer, bb, 0, h))
    return pl.pallas_call(
        _attn_s_kernel,
        out_shape=jax.ShapeDtypeStruct((b, tp, w), BF16),
        grid=(b, nh),
        in_specs=[new, past, past, new, new,
                  pl.BlockSpec((tp, lp), lambda bb, h: (0, 0)),
                  pl.BlockSpec((tp, tp), lambda bb, h: (0, 0))],
        out_specs=new,
        compiler_params=_cp("parallel", "parallel"),
        name="attn_sample",
    )(qn, kpast, vpast, kb, vb, bp, bn)


def _seg_sum(x, e_ref):
    nb = x.shape[1] // LANE
    e = e_ref[...]
    return jnp.concatenate(
        [jnp.dot(x[:, i * LANE:(i + 1) * LANE], e, precision=HI, preferred_element_type=F32) for i in range(nb)],
        axis=1)


def _rwkv_prep_kernel(r_ref, k_ref, v_ref, x_ref, sr_ref, sk_ref, sv_ref, sx_ref,
                      mr_ref, mk_ref, mv_ref, mx_ref, w0_ref, w2_ref, a0_ref, a2_ref, g2_ref,
                      kk_ref, ka_ref, e_ref,
                      ro_ref, wo_ref, ko_ref, vo_ref, zo_ref, bo_ref, go_ref,
                      pr_ref, pk_ref, pv_ref, px_ref, *, tb, n_valid, masked):
    it = pl.program_id(1)

    @pl.when(it == 0)
    def _():
        pr_ref[...] = sr_ref[...]
        pk_ref[...] = sk_ref[...]
        pv_ref[...] = sv_ref[...]
        px_ref[...] = sx_ref[...]

    first = lax.broadcasted_iota(jnp.int32, (tb, 1), 0) == 0

    def shifted(x_ref_, prev_ref, mu_ref):
        x = x_ref_[...]
        prev = jnp.where(first, prev_ref[...], pltpu.roll(x, 1, axis=0))
        prev_ref[...] = x[tb - 1:tb, :]
        return x + mu_ref[...] * (prev - x)

    r = shifted(r_ref, pr_ref, mr_ref)
    k = shifted(k_ref, pk_ref, mk_ref)
    v = shifted(v_ref, pv_ref, mv_ref)
    x = shifted(x_ref, px_ref, mx_ref)
    lo = x[:, 0:LANE]
    hi = x[:, LANE:]
    mw = jnp.dot(jnp.tanh(lo).astype(BF16), w2_ref[...], preferred_element_type=F32)
    ma = jnp.dot(lo.astype(BF16), a2_ref[...], preferred_element_type=F32)
    g = jnp.dot(_sigmoid(hi).astype(BF16), g2_ref[...], preferred_element_type=F32)
    log_w = -_softplus(-(w0_ref[...] + mw)) - 0.5
    decay = jnp.exp(-jnp.exp(log_w))
    a = _sigmoid(a0_ref[...] + ma)
    kk = k * kk_ref[...]
    kk = kk / jnp.maximum(jnp.sqrt(_seg_sum(kk * kk, e_ref)), 1e-12)
    k2 = k * (1.0 + (a - 1.0) * ka_ref[...])
    bvec = kk * a
    if masked:
        live = (it * tb + lax.broadcasted_iota(jnp.int32, (tb, 1), 0)) < n_valid
        decay = jnp.where(live, decay, 1.0)
        k2 = jnp.where(live, k2, 0.0)
        bvec = jnp.where(live, bvec, 0.0)
    for ref, val in ((ro_ref, r), (wo_ref, decay), (ko_ref, k2), (zo_ref, -kk), (bo_ref, bvec)):
        for c in range(ref.shape[0]):
            ref[c] = val
    vo_ref[...] = v
    go_ref[...] = g


def _rwkv_prep(proj3, shift, pw, w, xw, n_valid, ncopy):
    b, t, _ = proj3.shape
    tb = _pick(t, TIME_BLOCK_CAP, SUBLANE)
    xcol = (12 * w) // xw
    big = lambda c: pl.BlockSpec((None, tb, w), lambda bb, i: (bb, i, c))
    st = lambda n: pl.BlockSpec((None, 1, n), lambda bb, i: (bb, 0, 0))
    full = lambda shape: pl.BlockSpec(shape, lambda bb, i: (0,) * len(shape))
    one = pl.BlockSpec((None, tb, w), lambda bb, i: (bb, i, 0))
    cop = pl.BlockSpec((ncopy, None, tb, w), lambda bb, i: (0, bb, i, 0))
    one_s = jax.ShapeDtypeStruct((b, t, w), F32)
    cop_s = jax.ShapeDtypeStruct((ncopy, b, t, w), F32)
    sr, sk, sv, sx = shift
    r4, w4, k4, v3, z4, b4, g3 = pl.pallas_call(
        functools.partial(_rwkv_prep_kernel, tb=tb, n_valid=n_valid, masked=n_valid < t),
        out_shape=(cop_s, cop_s, cop_s, one_s, cop_s, cop_s, one_s),
        grid=(b, t // tb),
        in_specs=[big(9), big(10), big(11), pl.BlockSpec((None, tb, xw), lambda bb, i: (bb, i, xcol)),
                  st(w), st(w), st(w), st(xw),
                  full((1, w)), full((1, w)), full((1, w)), full((1, xw)),
                  full((1, w)), full((LANE, w)), full((1, w)), full((LANE, w)), full((xw - LANE, w)),
                  full((1, w)), full((1, w)), full((LANE, LANE))],
        out_specs=(cop, cop, cop, one, cop, cop, one),
        scratch_shapes=[pltpu.VMEM((1, w), F32)] * 3 + [pltpu.VMEM((1, xw), F32)],
        compiler_params=_cp("parallel", "arbitrary"),
        name="rwkv_prep",
    )(proj3, proj3, proj3, proj3, sr, sk, sv, sx, *pw)
    return r4, w4, k4, z4, b4, v3, g3


def _rwkv_scan_kernel(r_ref, w_ref, k_ref, z_ref, b_ref, v_ref, s0_ref, o_ref, so_ref, s_ref, *, tc):
    nk, nv = s_ref.shape[0], s_ref.shape[1]
    vg = min(nv, SCAN_V_GROUP)
    groups = [slice(g0, g0 + vg) for g0 in range(0, nv, vg)]
    kc = min(nk, SCAN_K_CHUNK)
    zero = jnp.zeros((vg, LANE), F32)

    @pl.when(pl.program_id(0) == 0)
    def _():
        s_ref[...] = s0_ref[...]

    def tree(parts):
        parts = list(parts)
        while len(parts) > 1:
            parts = [sum(parts[i:i + 2][1:], parts[i]) for i in range(0, len(parts), 2)]
        return parts[0]

    def step(t, carry):
        def row(ref, kx):
            return ref[t, pl.ds(kx, 1), :]

        def pass1(g, c, acc):
            acc = list(acc)
            for j in range(kc):
                kx = c * kc + j
                acc[j % 4] = acc[j % 4] + s_ref[kx, g, :] * row(z_ref, kx)
            return tuple(acc)

        def pass2(g, sz, vv, c, acc):
            acc = list(acc)
            for j in range(kc):
                kx = c * kc + j
                sn = s_ref[kx, g, :] * row(w_ref, kx) + sz * row(b_ref, kx) + vv * row(k_ref, kx)
                s_ref[kx, g, :] = sn
                acc[j % 4] = acc[j % 4] + sn * row(r_ref, kx)
            return tuple(acc)

        for g in groups:
            sz = tree(lax.fori_loop(0, nk // kc, functools.partial(pass1, g), (zero,) * 4))
            acc = lax.fori_loop(0, nk // kc, functools.partial(pass2, g, sz, v_ref[t, g, :]), (zero,) * 4)
            o_ref[t, g, :] = tree(acc)
        return carry

    lax.fori_loop(0, tc, step, 0)
    so_ref[...] = s_ref[...]


def _rwkv_scan(rl, wl, kl, zl, bl, vl, s0l):
    t, nk, _ = rl.shape
    nv = vl.shape[1]
    tc = _pick(t, SCAN_STEPS_CAP, 1)
    tok = pl.BlockSpec((tc, nk, LANE), lambda i: (i, 0, 0))
    vsp = pl.BlockSpec((tc, nv, LANE), lambda i: (i, 0, 0))
    ssp = pl.BlockSpec((nk, nv, LANE), lambda i: (0, 0, 0))
    return pl.pallas_call(
        functools.partial(_rwkv_scan_kernel, tc=tc),
        out_shape=(jax.ShapeDtypeStruct((t, nv, LANE), F32), jax.ShapeDtypeStruct((nk, nv, LANE), F32)),
        grid=(t // tc,),
        in_specs=[tok, tok, tok, tok, tok, vsp, ssp],
        out_specs=(vsp, ssp),
        scratch_shapes=[pltpu.VMEM((nk, nv, LANE), F32)],
        compiler_params=_cp("arbitrary"),
        name="rwkv_scan",
    )(rl, wl, kl, zl, bl, vl, s0l)


def _rwkv_post_kernel(o_ref, r_ref, k_ref, v_ref, g_ref, lg_ref, lb_ref, rk_ref, e_ref, y_ref):
    o = o_ref[...]
    inv_n = 1.0 / D_HD
    mu = _seg_sum(o, e_ref) * inv_n
    d = o - mu
    var = _seg_sum(d * d, e_ref) * inv_n
    on = d * lax.rsqrt(var + RWKV_GN_EPS) * lg_ref[...] + lb_ref[...]
    bonus = _seg_sum(r_ref[...] * k_ref[...] * rk_ref[...], e_ref) * v_ref[...]
    y_ref[...] = ((on + bonus) * g_ref[...]).astype(y_ref.dtype)


def _rwkv_post(o3, r3, k3, v3, g3, lg, lb, rk, e):
    b, t, w = o3.shape
    tb = _pick(t, TIME_BLOCK_CAP, SUBLANE)
    big = pl.BlockSpec((None, tb, w), lambda bb, i: (bb, i, 0))
    rowsp = pl.BlockSpec((1, w), lambda bb, i: (0, 0))
    return pl.pallas_call(
        _rwkv_post_kernel,
        out_shape=jax.ShapeDtypeStruct((b, t, w), BF16),
        grid=(b, t // tb),
        in_specs=[big] * 5 + [rowsp] * 3 + [pl.BlockSpec((LANE, LANE), lambda bb, i: (0, 0))],
        out_specs=big,
        compiler_params=_cp("parallel", "parallel"),
        name="rwkv_post",
    )(o3, r3, k3, v3, g3, lg, lb, rk, e)


def _rwkv(proj3, shift, s0, pw, w, xw, n_valid):
    b, t, _ = proj3.shape
    nh = w // D_HD
    chains = b * nh
    assert LANE % chains == 0 and D_HD % (LANE // chains) == 0
    vh = LANE // chains
    nv = D_HD // vh
    prep_w, (lg, lb, rk, e) = pw
    r4, w4, k4, z4, b4, v3, g3 = _rwkv_prep(proj3, shift, prep_w, w, xw, n_valid, vh)
    r3, k3 = r4[0], k4[0]

    def klay(x4):
        return x4.reshape(vh, b, t, nh, D_HD).transpose(2, 4, 0, 1, 3).reshape(t, D_HD, LANE)

    def vlay(x3):
        return x3.reshape(b, t, nh, vh, nv).transpose(1, 4, 3, 0, 2).reshape(t, nv, LANE)

    s0l = s0.reshape(b, nh, vh, nv, D_HD).transpose(4, 3, 2, 0, 1).reshape(D_HD, nv, LANE)
    ol, sl = _rwkv_scan(klay(r4), klay(w4), klay(k4), klay(z4), klay(b4), vlay(v3), s0l)
    o3 = ol.reshape(t, nv, vh, b, nh).transpose(3, 0, 4, 2, 1).reshape(b, t, w)
    s_new = sl.reshape(D_HD, nv, vh, b, nh).transpose(3, 4, 2, 1, 0).reshape(b, nh, D_HD, D_HD)
    y = _rwkv_post(o3, r3, k3, v3, g3, lg, lb, rk, e)
    return y, s_new


def _layer(x3, n_valid, tables, state, lbv, p, dims):
    d, w, xw, n_in = dims
    b, t, _ = x3.shape
    k_past, v_past, s_a, h_b, buf_b, s_d, shift_d, buf_f = state
    m = b * t
    x2 = x3.reshape(m, d)
    h = _rmsnorm_bf16(x2, p['ln_mix_g'])
    big, layer = p['big'], p['layer']
    proj = _matmul(h, big['w_in'], layer, tm_cap=MM_TM_CAP, tn_cap=1152, tk_cap=d, w_resident=True)
    proj3 = proj.reshape(b, t, -1)
    y_a, s_a_t = _hgrn(proj3, lbv, p['hgrn_norm_g'].astype(F32), jnp.swapaxes(s_a, -1, -2), w, n_valid)
    s_a_new = jnp.swapaxes(s_a_t, -1, -2)
    y_b, h_b_new, buf_b_new = _lru(proj3, buf_b, h_b, p, w, n_valid)
    qn, kn, v_new, kb, vb = _qk_prep(proj3, p['attn_q_norm_g'], p['attn_k_norm_g'], tables, w)
    if k_past is None:
        y_c = _attn_prompt(qn, kb, vb, w)
    else:
        (k_all, layer), (v_all, _) = k_past, v_past
        y_c = _attn_sample(qn, k_all, v_all, layer, kb, vb, w, n_valid)
    sh = shift_d.reshape(b, 1, -1)
    shift = (sh[:, :, 0:w], sh[:, :, w:2 * w], sh[:, :, 2 * w:3 * w],
             jnp.pad(sh[:, :, 3 * w:], ((0, 0), (0, 0), (0, xw - (sh.shape[2] - 3 * w)))))
    y_d, s_d_new = _rwkv(proj3, shift, s_d, p['rwkv'], w, xw, n_valid)
    shift_new = proj3[:, n_valid - 1, 9 * w:n_in]
    y_mix = jnp.concatenate([y_a, y_b, y_c, y_d], axis=-1).reshape(m, 4 * w)
    x2 = _matmul(y_mix, big['w_out'], layer, x2, tm_cap=MM_TM_CAP, tn_cap=512, tk_cap=4 * w, w_resident=True)
    h = _rmsnorm_bf16(x2, p['ln_ffn_g'])
    z, buf_f_new = _ffn_gate_up(h.reshape(b, t, d), big['ffn_w_gate'], big['ffn_w_up'], layer, p['ffn_conv_w'],
                                buf_f, n_valid)
    f = z.shape[-1]
    x2 = _matmul(z.reshape(m, f), big['ffn_w_down'], layer, x2, tm_cap=MM_TM_CAP, tn_cap=512, tk_cap=f // 2,
                 w_resident=False)
    keep = slice(max(n_valid - C_WIN, 0), n_valid) if k_past is None else slice(0, n_valid)
    new_state = (kn[:, keep], v_new[:, keep], s_a_new, h_b_new.reshape(b, w), buf_b_new,
                 s_d_new, shift_new, buf_f_new)
    return x2.reshape(b, t, d), new_state


def kernel(x_prompt, x_sample, cache_attn_k, cache_attn_v, state_hgrn, state_lru_h, state_lru_conv, state_rwkv, state_rwkv_shift, state_ffn_conv, ln_mix_g, w_in, hgrn_lb_logits, hgrn_norm_g, lru_conv_w, lru_conv_b, lru_wa, lru_ba, lru_wx, lru_bx, lru_lambda, attn_q_norm_g, attn_k_norm_g, rwkv_mu, rwkv_w0, rwkv_w2, rwkv_a0, rwkv_a2, rwkv_g2, rwkv_k_k, rwkv_k_a, rwkv_r_k, rwkv_ln_g, rwkv_ln_b, w_out, ln_ffn_g, ffn_w_gate, ffn_w_up, ffn_conv_w, ffn_w_down):
    bp, tp, d = x_prompt.shape
    bs, ts, _ = x_sample.shape
    depth = w_in.shape[0]
    n_in = w_in.shape[2]
    w = d // 4
    n_pad = -(-n_in // LANE) * LANE
    rank_w, rank_a, rank_g = rwkv_w2.shape[1], rwkv_a2.shape[1], rwkv_g2.shape[1]
    assert rank_w + rank_a == LANE
    xw = n_pad - 12 * w
    dims = (d, w, xw, n_in)
    row = lambda v: v.reshape(1, -1).astype(F32)

    lb_soft = jax.nn.softmax(hgrn_lb_logits.astype(F32), axis=0)
    lb_all = jnp.cumsum(lb_soft, axis=0) - lb_soft[:1]

    big = {
        'w_in': jnp.pad(w_in, ((0, 0), (0, 0), (0, n_pad - n_in))).astype(BF16),
        'w_out': w_out.astype(BF16),
        'ffn_w_gate': ffn_w_gate.astype(BF16), 'ffn_w_up': ffn_w_up.astype(BF16),
        'ffn_w_down': ffn_w_down.astype(BF16),
    }
    zeros = lambda r: jnp.zeros((r, w), F32)
    head_id = np.arange(LANE) // D_HD
    e_blk = jnp.asarray((head_id[:, None] == head_id[None, :]).astype(np.float32))

    layers = []
    for l in range(depth):
        mu = rwkv_mu[l].astype(F32)
        prep_w = (
            row(mu[0:w]), row(mu[w:2 * w]), row(mu[2 * w:3 * w]),
            row(jnp.pad(mu[3 * w:], (0, xw - (mu.shape[0] - 3 * w)))),
            row(rwkv_w0[l]),
            jnp.concatenate([rwkv_w2[l].astype(F32), zeros(rank_a)], axis=0).astype(BF16),
            row(rwkv_a0[l]),
            jnp.concatenate([zeros(rank_w), rwkv_a2[l].astype(F32)], axis=0).astype(BF16),
            jnp.concatenate([rwkv_g2[l].astype(F32), zeros(xw - LANE - rank_g)], axis=0).astype(BF16),
            row(rwkv_k_k[l]), row(rwkv_k_a[l]), e_blk,
        )
        post_w = (row(rwkv_ln_g[l]), row(rwkv_ln_b[l]), row(rwkv_r_k[l]), e_blk)
        layers.append({
            'ln_mix_g': ln_mix_g[l], 'big': big, 'layer': l, 'hgrn_norm_g': hgrn_norm_g[l],
            'lru_conv_w': lru_conv_w[l], 'lru_conv_b': lru_conv_b[l], 'lru_wa': lru_wa[l],
            'lru_ba': lru_ba[l], 'lru_wx': lru_wx[l], 'lru_bx': lru_bx[l], 'lru_lambda': lru_lambda[l],
            'attn_q_norm_g': attn_q_norm_g[l], 'attn_k_norm_g': attn_k_norm_g[l],
            'rwkv': (prep_w, post_w), 'ln_ffn_g': ln_ffn_g[l], 'ffn_conv_w': ffn_conv_w[l].astype(F32),
        })

    f = ffn_w_gate.shape[2]
    nh_a, nh_d = w // A_HD, w // D_HD
    zero_state = (None, None, jnp.zeros((bp, nh_a, A_HD, A_HD), F32), jnp.zeros((bp, w), F32),
                  jnp.zeros((bp, 3, w), F32), jnp.zeros((bp, nh_d, D_HD, D_HD), F32),
                  jnp.zeros((bp, n_in - 9 * w), F32), jnp.zeros((bp, 2, f), F32))
    tab_p = _rope_tables(jnp.arange(tp, dtype=jnp.int32))
    tab_s = _rope_tables(PAST_LEN + jnp.arange(PAD_T, dtype=jnp.int32))
    lp = cache_attn_k.shape[2]
    cache_k = cache_attn_k.reshape(depth, bs, lp, w)
    cache_v = cache_attn_v.reshape(depth, bs, lp, w)
    xp = x_prompt.astype(F32)
    xs = jnp.pad(x_sample.astype(F32), ((0, 0), (0, PAD_T - ts), (0, 0)))
    p_states, s_states = [], []
    for l in range(depth):
        xp, st_p = _layer(xp, tp, tab_p, zero_state, lb_all[l], layers[l], dims)
        p_states.append(st_p)
        s_in = ((cache_k, l), (cache_v, l), state_hgrn[l].astype(F32),
                state_lru_h[l].astype(F32), state_lru_conv[l].astype(F32), state_rwkv[l].astype(F32),
                state_rwkv_shift[l].astype(F32), state_ffn_conv[l].astype(F32))
        xs, st_s = _layer(xs, ts, tab_s, s_in, lb_all[l], layers[l], dims)
        s_states.append(st_s)

    dp, ds = x_prompt.dtype, x_sample.dtype
    nh_c = w // C_HD

    def stack(states, i, dt, shape=None):
        out = jnp.stack([st[i] for st in states]).astype(dt)
        return out if shape is None else out.reshape((depth,) + shape)

    outs = [xp.astype(dp), xs[:, :ts].astype(ds)]
    for states, bb, tt, dt in ((p_states, bp, min(C_WIN, tp), dp), (s_states, bs, ts, ds)):
        outs += [stack(states, 0, dt, (bb, tt, nh_c, C_HD)), stack(states, 1, dt, (bb, tt, nh_c, C_HD)),
                 stack(states, 2, dt), stack(states, 3, dt), stack(states, 4, dt), stack(states, 5, dt),
                 stack(states, 6, dt), stack(states, 7, dt)]
    return tuple(outs)
```

```python
import functools
import math

import numpy as np
import jax
import jax.numpy as jnp
from jax import lax
from jax.experimental import pallas as pl
from jax.experimental.pallas import tpu as pltpu

F32 = jnp.float32
BF16 = jnp.bfloat16

PAST_LEN = 8192
A_HD = 128
B_HD = 128
C_HD = 128
D_HD = 64
ROPE_DIM = C_HD // 4
ROPE_THETA = 500000.0
C_PATTERNS = ((128, 1), (512, 4), (2048, 16))
LRU_C = 8.0
RMS_EPS = 1e-6
RWKV_GN_EPS = 64e-5
NEG_BIG = -30000.0
F_FLOOR = 1e-20

LANE = 128
SUBLANE = 8
VMEM_LIMIT_BYTES = 56 * 2**20

C_WIN = max(win for win, _ in C_PATTERNS)

TIME_BLOCK_CAP = 256
MM_TM_CAP = 1024
FFN_TN = 256
FFN_ROW_PIECES = 1
SCAN_STEPS_CAP = 32
SCAN_V_GROUP = 32
SCAN_K_CHUNK = 32

HGRN_CHUNK = 16
HGRN_ROW_GROUP = 64
PAD_T = 16
ATT_BLK = 256
MASK_BIAS = -1e30
HI = lax.Precision.HIGHEST


def _cp(*sem):
    return pltpu.CompilerParams(dimension_semantics=sem, vmem_limit_bytes=VMEM_LIMIT_BYTES)


def _pick(n, cap, quantum):
    if n <= cap:
        return n
    t = (cap // quantum) * quantum
    while t >= quantum:
        if n % t == 0:
            return t
        t -= quantum
    raise ValueError(f"no tile for {n} (cap {cap}, quantum {quantum})")


def _sigmoid(x):
    return jax.nn.sigmoid(x)


def _silu(x):
    return x * jax.nn.sigmoid(x)


def _softplus(x):
    return jnp.maximum(x, 0.0) + jnp.log(1.0 + jnp.exp(-jnp.abs(x)))


def _gelu_tanh(x):
    return x * (0.5 * (1.0 + jnp.tanh(math.sqrt(2.0 / math.pi) * (x + 0.044715 * (x * x * x)))))


def _norm_kernel(x_ref, g_ref, o_ref):
    x = x_ref[...]
    inv = lax.rsqrt(jnp.mean(x * x, axis=-1, keepdims=True) + RMS_EPS)
    o_ref[...] = (x * inv * g_ref[...]).astype(o_ref.dtype)


def _rmsnorm_bf16(x2, g):
    m, d = x2.shape
    tm = _pick(m, 256, SUBLANE)
    return pl.pallas_call(
        _norm_kernel,
        out_shape=jax.ShapeDtypeStruct((m, d), BF16),
        grid=(m // tm,),
        in_specs=[pl.BlockSpec((tm, d), lambda i: (i, 0)), pl.BlockSpec((1, d), lambda i: (0, 0))],
        out_specs=pl.BlockSpec((tm, d), lambda i: (i, 0)),
        compiler_params=_cp("parallel"),
        name="rmsnorm",
    )(x2, g.reshape(1, d).astype(F32))


def _mm_kernel(*refs, nk, has_res):
    if has_res:
        a_ref, w_ref, r_ref, o_ref = refs[:4]
    else:
        a_ref, w_ref, o_ref = refs[:3]
        r_ref = None
    part = jnp.dot(a_ref[...], w_ref[...], preferred_element_type=F32)
    if nk == 1:
        o_ref[...] = (r_ref[...] + part) if has_res else part
        return
    acc_ref = refs[-1]
    k = pl.program_id(2)

    @pl.when(k == 0)
    def _():
        acc_ref[...] = part

    @pl.when(k > 0)
    def _():
        acc_ref[...] += part

    @pl.when(k == nk - 1)
    def _():
        o_ref[...] = (r_ref[...] + acc_ref[...]) if has_res else acc_ref[...]


def _matmul(a, w, layer, res=None, *, tm_cap, tn_cap, tk_cap, w_resident):
    m, k = a.shape
    n = w.shape[2]
    tm = _pick(m, tm_cap, SUBLANE)
    tn = _pick(n, tn_cap, LANE)
    tk = _pick(k, tk_cap, LANE)
    nk = k // tk
    if w_resident:
        grid = (n // tn, m // tm, nk)
        amap = lambda j, i, kk: (i, kk)
        wmap = lambda j, i, kk: (layer, kk, j)
        omap = lambda j, i, kk: (i, j)
    else:
        grid = (m // tm, n // tn, nk)
        amap = lambda i, j, kk: (i, kk)
        wmap = lambda i, j, kk: (layer, kk, j)
        omap = lambda i, j, kk: (i, j)
    in_specs = [pl.BlockSpec((tm, tk), amap), pl.BlockSpec((None, tk, tn), wmap)]
    args = [a, w]
    if res is not None:
        in_specs.append(pl.BlockSpec((tm, tn), omap))
        args.append(res)
    scratch = [pltpu.VMEM((tm, tn), F32)] if nk > 1 else []
    return pl.pallas_call(
        functools.partial(_mm_kernel, nk=nk, has_res=res is not None),
        out_shape=jax.ShapeDtypeStruct((m, n), F32),
        grid=grid,
        in_specs=in_specs,
        out_specs=pl.BlockSpec((tm, tn), omap),
        scratch_shapes=scratch,
        compiler_params=_cp("parallel", "parallel", "arbitrary"),
        name="matmul_res" if res is not None else "matmul",
    )(*args)


def _ffn_kernel(h_ref, wg_ref, wu_ref, cw_ref, st_ref, z_ref, so_ref, gs_ref, *, nseq, ts, nsub, i_state,
                r_state):
    i = pl.program_id(2)

    @pl.when(i == 0)
    def _():
        for s in range(nseq):
            gs_ref[s, 6:8, :] = st_ref[s]

    @pl.when(i > 0)
    def _():
        for s in range(nseq):
            gs_ref[s, 0:8, :] = gs_ref[s, ts:ts + 8, :]

    cw = cw_ref[...]
    piece = ts // nsub
    for s in range(nseq):
        for r in range(nsub):
            lo = r * piece
            rows = slice(s * ts + lo, s * ts + lo + piece)
            h = h_ref[rows, :]
            g = jnp.dot(h, wg_ref[...], preferred_element_type=F32)
            up = jnp.dot(h, wu_ref[...], preferred_element_type=F32)
            gs_ref[s, 8 + lo:8 + lo + piece, :] = g
            y = gs_ref[s, 6 + lo:6 + lo + piece, :] * cw[0:1, :]
            y = y + gs_ref[s, 7 + lo:7 + lo + piece, :] * cw[1:2, :]
            y = y + g * cw[2:3, :]
            z_ref[rows, :] = (_silu(y) * up).astype(z_ref.dtype)

    @pl.when(i == i_state)
    def _():
        for s in range(nseq):
            so_ref[s] = gs_ref[s, 8 + r_state:10 + r_state, :]


def _ffn_gate_up(h3, wg, wu, layer, cw, st, n_valid):
    b, t, d = h3.shape
    f = wg.shape[2]
    tn = _pick(f, FFN_TN, LANE)
    if b * t <= MM_TM_CAP:
        nseq, ts, nb, nt = b, t, 1, 1
    else:
        nseq, ts, nb, nt = 1, _pick(t, MM_TM_CAP, SUBLANE), b, t // _pick(t, MM_TM_CAP, SUBLANE)
    tm = nseq * ts
    i_state, r_state = divmod(n_valid - 2, ts)
    assert r_state + 2 <= ts
    z, so = pl.pallas_call(
        functools.partial(_ffn_kernel, nseq=nseq, ts=ts, nsub=FFN_ROW_PIECES if nseq == 1 and ts % 32 == 0 else 1,
                          i_state=i_state, r_state=r_state),
        out_shape=(jax.ShapeDtypeStruct((nb, nt * tm, f), BF16), jax.ShapeDtypeStruct((b, 2, f), F32)),
        grid=(f // tn, nb, nt),
        in_specs=[
            pl.BlockSpec((None, tm, d), lambda j, bb, i: (bb, i, 0)),
            pl.BlockSpec((None, d, tn), lambda j, bb, i: (layer, 0, j)),
            pl.BlockSpec((None, d, tn), lambda j, bb, i: (layer, 0, j)),
            pl.BlockSpec((3, tn), lambda j, bb, i: (0, j)),
            pl.BlockSpec((nseq, 2, tn), lambda j, bb, i: (bb, 0, j)),
        ],
        out_specs=(
            pl.BlockSpec((None, tm, tn), lambda j, bb, i: (bb, i, j)),
            pl.BlockSpec((nseq, 2, tn), lambda j, bb, i: (bb, 0, j)),
        ),
        scratch_shapes=[pltpu.VMEM((nseq, ts + 8, tn), F32)],
        compiler_params=_cp("parallel", "parallel", "arbitrary"),
        name="ffn_gate_up",
    )(h3.reshape(nb, nt * tm, d), wg, wu, cw, st)
    return z.reshape(b, t, f), so


def _hgrn_kernel(q_ref, f_ref, i_ref, g_ref, lb_ref, ng_ref, tri_ref, s0_ref, y_ref, so_ref,
                 st_ref, kp_ref, gp_ref, vp_ref, o_ref, qs_ref, ud_ref, *, tb, n_valid, masked, n_chunks):
    it = pl.program_id(2)
    ch = HGRN_CHUNK

    @pl.when(it == 0)
    def _():
        st_ref[...] = s0_ref[...]

    lb = lb_ref[...]
    fl = f_ref[...]
    f = lb + (1.0 - lb) * _sigmoid(fl)
    lf = jnp.log(jnp.maximum(f, F_FLOOR))
    kk = (1.0 - lb) * _sigmoid(-fl)
    row = lax.broadcasted_iota(jnp.int32, (tb, 1), 0)
    if masked:
        live = (it * tb + row) < n_valid
        lf = jnp.where(live, lf, 0.0)
        kk = jnp.where(live, kk, 0.0)
    q = _silu(q_ref[...]) * (A_HD ** -0.5)
    v = i_ref[...]
    g = jnp.dot(tri_ref[...], lf, precision=HI, preferred_element_type=F32)
    zero = jnp.zeros((ch, A_HD), F32)
    for ref, val in ((kp_ref, kk), (gp_ref, g), (vp_ref, v)):
        ref[0:ch, :] = zero
        ref[ch:tb + ch, :] = val
    qs_ref[...] = q
    qe = (q * jnp.exp(g)).astype(BF16)

    for c in range(n_chunks):
        lo = c * ch
        gl = gp_ref[lo + 2 * ch - 1:lo + 2 * ch, :]
        kt = kk[lo:lo + ch, :] * jnp.exp(gl - g[lo:lo + ch, :])
        ud_ref[c] = lax.dot_general(v[lo:lo + ch, :].astype(BF16), kt.astype(BF16),
                                    (((0,), (0,)), ((), ())), preferred_element_type=F32)
    st = st_ref[...]
    for c in range(n_chunks):
        lo = c * ch
        o_ref[lo:lo + ch, :] = lax.dot_general(qe[lo:lo + ch, :], st.astype(BF16),
                                               (((1,), (1,)), ((), ())), preferred_element_type=F32)
        st = st * jnp.exp(gp_ref[lo + 2 * ch - 1:lo + 2 * ch, :]) + ud_ref[c]
    st_ref[...] = st
    so_ref[...] = st

    grp = min(tb, HGRN_ROW_GROUP)
    pos = lax.broadcasted_iota(jnp.int32, (grp, 1), 0) & (ch - 1)
    for r0 in range(0, tb, grp):
        qg = qs_ref[r0:r0 + grp, :]
        gg = gp_ref[ch + r0:ch + r0 + grp, :]
        o = o_ref[r0:r0 + grp, :]
        for d in range(ch):
            lo = ch + r0 - d
            dec = jnp.exp(jnp.where(pos >= d, gg - gp_ref[lo:lo + grp, :], NEG_BIG))
            a = jnp.sum(qg * kp_ref[lo:lo + grp, :] * dec, axis=-1, keepdims=True)
            o = o + a * vp_ref[lo:lo + grp, :]
        o = o * lax.rsqrt(jnp.mean(o * o, axis=-1, keepdims=True) + RMS_EPS) * ng_ref[...]
        y_ref[r0:r0 + grp, :] = (o * _silu(g_ref[r0:r0 + grp, :])).astype(y_ref.dtype)


def _hgrn(proj3, lb, norm_g, s0_t, w, n_valid):
    b, t, _ = proj3.shape
    nh = w // A_HD
    tb = _pick(t, TIME_BLOCK_CAP, HGRN_CHUNK)
    ch = HGRN_CHUNK
    assert ch & (ch - 1) == 0
    idx = np.arange(tb)
    tri = ((idx[:, None] // ch == idx[None, :] // ch) & (idx[None, :] <= idx[:, None])).astype(np.float32)

    def col(off):
        return lambda bb, h, i: (bb, i, off * nh + h)

    return pl.pallas_call(
        functools.partial(_hgrn_kernel, tb=tb, n_valid=n_valid, masked=n_valid < t, n_chunks=tb // ch),
        out_shape=(jax.ShapeDtypeStruct((b, t, w), BF16), jax.ShapeDtypeStruct((b, nh, A_HD, A_HD), F32)),
        grid=(b, nh, t // tb),
        in_specs=[
            pl.BlockSpec((None, tb, A_HD), col(0)),
            pl.BlockSpec((None, tb, A_HD), col(1)),
            pl.BlockSpec((None, tb, A_HD), col(2)),
            pl.BlockSpec((None, tb, A_HD), col(3)),
            pl.BlockSpec((1, A_HD), lambda bb, h, i: (0, h)),
            pl.BlockSpec((1, A_HD), lambda bb, h, i: (0, 0)),
            pl.BlockSpec((tb, tb), lambda bb, h, i: (0, 0)),
            pl.BlockSpec((None, None, A_HD, A_HD), lambda bb, h, i: (bb, h, 0, 0)),
        ],
        out_specs=(
            pl.BlockSpec((None, tb, A_HD), lambda bb, h, i: (bb, i, h)),
            pl.BlockSpec((None, None, A_HD, A_HD), lambda bb, h, i: (bb, h, 0, 0)),
        ),
        scratch_shapes=[pltpu.VMEM((A_HD, A_HD), F32)] + [pltpu.VMEM((tb + ch, A_HD), F32)] * 3
        + [pltpu.VMEM((tb, A_HD), F32)] * 2 + [pltpu.VMEM((tb // ch, A_HD, A_HD), F32)],
        compiler_params=_cp("parallel", "parallel", "arbitrary"),
        name="hgrn2",
    )(proj3, proj3, proj3, proj3, lb.reshape(1, w), norm_g.reshape(1, A_HD), jnp.asarray(tri), s0_t)


def _lru_kernel(x_ref, g_ref, buf_ref, h0_ref, cw_ref, cb_ref, wa_ref, ba_ref, wx_ref, bx_ref, lam_ref,
                y_ref, hl_ref, bo_ref, xs_ref, a_ref, b_ref, hc_ref, *, tb, n_valid, masked, i_state, r_state):
    it = pl.program_id(1)
    nh = wa_ref.shape[0]

    @pl.when(it == 0)
    def _():
        xs_ref[5:8, :] = buf_ref[...]
        hc_ref[...] = h0_ref[...]

    @pl.when(it > 0)
    def _():
        xs_ref[0:8, :] = xs_ref[tb:tb + 8, :]

    x = x_ref[...]
    xs_ref[8:tb + 8, :] = x
    cw = cw_ref[...]
    xc = xs_ref[5:tb + 5, :] * cw[0:1, :]
    xc = xc + xs_ref[6:tb + 6, :] * cw[1:2, :]
    xc = xc + xs_ref[7:tb + 7, :] * cw[2:3, :]
    xc = xc + x * cw[3:4, :]
    xc = xc + cb_ref[...]
    xcb = xc.astype(BF16)
    ra, rx = [], []
    for h in range(nh):
        xh = xcb[:, h * B_HD:(h + 1) * B_HD]
        ra.append(jnp.dot(xh, wa_ref[h], preferred_element_type=F32))
        rx.append(jnp.dot(xh, wx_ref[h], preferred_element_type=F32))
    r = _sigmoid(jnp.concatenate(ra, axis=1) + ba_ref[...])
    ig = _sigmoid(jnp.concatenate(rx, axis=1) + bx_ref[...])
    log_a = (-LRU_C) * r * _softplus(-lam_ref[...])
    a = jnp.exp(log_a)
    om = -jnp.tanh(log_a) * (a * a + 1.0)
    bb = jnp.sqrt(jnp.maximum(om, 0.0)) * (ig * xc)
    if masked:
        row = it * tb + lax.broadcasted_iota(jnp.int32, (tb, 1), 0)
        live = row < n_valid
        a = jnp.where(live, a, 1.0)
        bb = jnp.where(live, bb, 0.0)
    a_ref[...] = a
    b_ref[...] = bb

    def step(t, h):
        h = a_ref[pl.ds(t, 1), :] * h + b_ref[pl.ds(t, 1), :]
        b_ref[pl.ds(t, 1), :] = h
        return h

    h = lax.fori_loop(0, tb, step, hc_ref[...], unroll=8)
    hc_ref[...] = h
    hl_ref[...] = h
    y_ref[...] = (b_ref[...] * _gelu_tanh(g_ref[...])).astype(y_ref.dtype)

    @pl.when(it == i_state)
    def _():
        bo_ref[...] = xs_ref[8 + r_state:11 + r_state, :]


def _lru(proj3, buf, h0, p, w, n_valid):
    b, t, _ = proj3.shape
    nh = w // B_HD
    tb = _pick(t, TIME_BLOCK_CAP, SUBLANE)
    i_state, r_state = divmod(n_valid - 3, tb)
    assert n_valid >= 3 and r_state + 3 <= tb
    row = lambda v: v.reshape(1, w).astype(F32)
    full = lambda shape: pl.BlockSpec(shape, lambda bb, i: (0,) * len(shape))
    return pl.pallas_call(
        functools.partial(_lru_kernel, tb=tb, n_valid=n_valid, masked=n_valid < t,
                          i_state=i_state, r_state=r_state),
        out_shape=(jax.ShapeDtypeStruct((b, t, w), BF16), jax.ShapeDtypeStruct((b, 1, w), F32),
                   jax.ShapeDtypeStruct((b, 3, w), F32)),
        grid=(b, t // tb),
        in_specs=[
            pl.BlockSpec((None, tb, w), lambda bb, i: (bb, i, 4)),
            pl.BlockSpec((None, tb, w), lambda bb, i: (bb, i, 5)),
            pl.BlockSpec((None, 3, w), lambda bb, i: (bb, 0, 0)),
            pl.BlockSpec((None, 1, w), lambda bb, i: (bb, 0, 0)),
            full((4, w)), full((1, w)), full((nh, B_HD, B_HD)), full((1, w)),
            full((nh, B_HD, B_HD)), full((1, w)), full((1, w)),
        ],
        out_specs=(
            pl.BlockSpec((None, tb, w), lambda bb, i: (bb, i, 0)),
            pl.BlockSpec((None, 1, w), lambda bb, i: (bb, 0, 0)),
            pl.BlockSpec((None, 3, w), lambda bb, i: (bb, 0, 0)),
        ),
        scratch_shapes=[pltpu.VMEM((tb + 8, w), F32), pltpu.VMEM((tb, w), F32), pltpu.VMEM((tb, w), F32),
                        pltpu.VMEM((1, w), F32)],
        compiler_params=_cp("parallel", "arbitrary"),
        name="rglru",
    )(proj3, proj3, buf, h0.reshape(b, 1, w), p['lru_conv_w'].astype(F32), row(p['lru_conv_b']),
      p['lru_wa'].astype(BF16), row(p['lru_ba']), p['lru_wx'].astype(BF16), row(p['lru_bx']),
      row(p['lru_lambda']))


def _qk_kernel(q_ref, k_ref, v_ref, qg_ref, kg_ref, c_ref, sa_ref, sb_ref,
               qo_ref, ko_ref, vo_ref, kb_ref, vb_ref):
    nh = q_ref.shape[1] // C_HD
    cosf, sina, sinb = c_ref[...], sa_ref[...], sb_ref[...]

    def prep(x, g):
        xn = x * lax.rsqrt(jnp.mean(x * x, axis=-1, keepdims=True) + RMS_EPS) * g
        up = pltpu.roll(xn, C_HD - ROPE_DIM // 2, axis=1)
        dn = pltpu.roll(xn, ROPE_DIM // 2, axis=1)
        return xn * cosf + up * sina + dn * sinb

    for h in range(nh):
        sl = slice(h * C_HD, (h + 1) * C_HD)
        qo_ref[:, sl] = (prep(q_ref[:, sl], qg_ref[...]) * (C_HD ** -0.5)).astype(qo_ref.dtype)
        kn = prep(k_ref[:, sl], kg_ref[...])
        ko_ref[:, sl] = kn
        kb_ref[:, sl] = kn.astype(kb_ref.dtype)
    v = v_ref[...]
    vo_ref[...] = v
    vb_ref[...] = v.astype(vb_ref.dtype)


def _rope_tables(pos):
    half = ROPE_DIM // 2
    inv = ROPE_THETA ** (-jnp.arange(half, dtype=F32) / half)
    ang = pos.astype(F32)[:, None] * inv[None, :]
    cos, sin = jnp.cos(ang), jnp.sin(ang)
    n = pos.shape[0]
    rest = C_HD - ROPE_DIM
    cosf = jnp.concatenate([cos, cos, jnp.ones((n, rest), F32)], axis=1)
    sina = jnp.concatenate([-sin, jnp.zeros((n, half + rest), F32)], axis=1)
    sinb = jnp.concatenate([jnp.zeros((n, half), F32), sin, jnp.zeros((n, rest), F32)], axis=1)
    return cosf, sina, sinb


def _qk_prep(proj3, qg, kg, tables, w):
    b, t, _ = proj3.shape
    tb = _pick(t, TIME_BLOCK_CAP, SUBLANE)
    tab = pl.BlockSpec((tb, C_HD), lambda bb, i: (i, 0))
    gsp = pl.BlockSpec((1, C_HD), lambda bb, i: (0, 0))
    src = lambda c: pl.BlockSpec((None, tb, w), lambda bb, i: (bb, i, c))
    dst = pl.BlockSpec((None, tb, w), lambda bb, i: (bb, i, 0))
    return pl.pallas_call(
        _qk_kernel,
        out_shape=(jax.ShapeDtypeStruct((b, t, w), BF16), jax.ShapeDtypeStruct((b, t, w), F32),
                   jax.ShapeDtypeStruct((b, t, w), F32),
                   jax.ShapeDtypeStruct((b, t, w), BF16), jax.ShapeDtypeStruct((b, t, w), BF16)),
        grid=(b, t // tb),
        in_specs=[src(6), src(7), src(8), gsp, gsp, tab, tab, tab],
        out_specs=(dst, dst, dst, dst, dst),
        compiler_params=_cp("parallel", "parallel"),
        name="qk_norm_rope",
    )(proj3, proj3, proj3, qg.reshape(1, C_HD).astype(F32), kg.reshape(1, C_HD).astype(F32), *tables)


def _pattern_bias(dist):
    dist = np.asarray(dist, np.int64)
    cnt = np.zeros(dist.shape, np.int64)
    for win, dil in C_PATTERNS:
        cnt += (dist >= 0) & (dist <= win) & (dist % dil == 0)
    return np.where(cnt > 0, np.log(np.maximum(cnt, 1)), MASK_BIAS).astype(np.float32)


def _attn_kernel(q_ref, k_ref, v_ref, bias_ref, o_ref, *, n_back, span, whole):
    qi = pl.program_id(2)

    def attend(rows, cols):
        s = lax.dot_general(q_ref[...], k_ref[rows, :], (((1,), (1,)), ((), ())), preferred_element_type=F32)
        s = s + bias_ref[:, cols]
        m = jnp.max(s, axis=-1, keepdims=True)
        p = jnp.exp(s - m)
        l = jnp.sum(p, axis=-1, keepdims=True)
        acc = jnp.dot(p.astype(BF16), v_ref[rows, :], preferred_element_type=F32)
        o_ref[...] = (acc / l).astype(o_ref.dtype)

    if whole:
        for n in range(1, span // ATT_BLK + 1):
            pl.when(qi == n - 1)(functools.partial(attend, slice(0, n * ATT_BLK), slice(0, n * ATT_BLK)))
    else:
        attend(pl.ds(pl.multiple_of(jnp.maximum(qi - n_back, 0) * ATT_BLK, ATT_BLK), span), slice(None))


def _attn_prompt(qn, kb, vb, w):
    b, t, _ = qn.shape
    nh = w // C_HD
    blk = ATT_BLK
    assert t % blk == 0
    n_back = -(-C_WIN // blk)
    n_span = min(n_back + 1, t // blk)
    r = np.arange(blk)
    n_tab = min(n_back, n_span - 1) + 1
    wide = jnp.asarray(_pattern_bias((n_tab - 1) * blk + r[:, None] - np.arange((n_tab - 1 + n_span) * blk)[None, :]))
    bias = jnp.stack([wide[:, (n_tab - 1 - k) * blk:(n_tab - 1 - k + n_span) * blk] for k in range(n_tab)])
    return pl.pallas_call(
        functools.partial(_attn_kernel, n_back=n_back, span=n_span * blk, whole=n_span == t // blk),
        out_shape=jax.ShapeDtypeStruct((b, t, w), BF16),
        grid=(b, nh, t // blk),
        in_specs=[
            pl.BlockSpec((None, blk, C_HD), lambda bb, h, i: (bb, i, h)),
            pl.BlockSpec((None, t, C_HD), lambda bb, h, i: (bb, 0, h)),
            pl.BlockSpec((None, t, C_HD), lambda bb, h, i: (bb, 0, h)),
            pl.BlockSpec((None, blk, n_span * blk), lambda bb, h, i: (jnp.minimum(i, n_tab - 1), 0, 0)),
        ],
        out_specs=pl.BlockSpec((None, blk, C_HD), lambda bb, h, i: (bb, i, h)),
        compiler_params=_cp("parallel", "parallel", "arbitrary"),
        name="attn_prompt",
    )(qn, kb, vb, bias)


def _attn_s_kernel(q_ref, kp_ref, vp_ref, kn_ref, vn_ref, bp_ref, bn_ref, o_ref):
    q = q_ref[...]
    nt = (((1,), (1,)), ((), ()))
    sp = lax.dot_general(q, kp_ref[...].astype(BF16), nt, preferred_element_type=F32) + bp_ref[...]
    sn = lax.dot_general(q, kn_ref[...], nt, preferred_element_type=F32) + bn_ref[...]
    m = jnp.maximum(jnp.max(sp, axis=-1, keepdims=True), jnp.max(sn, axis=-1, keepdims=True))
    pp = jnp.exp(sp - m)
    pn = jnp.exp(sn - m)
    l = jnp.sum(pp, axis=-1, keepdims=True) + jnp.sum(pn, axis=-1, keepdims=True)
    acc = jnp.dot(pp.astype(BF16), vp_ref[...].astype(BF16), preferred_element_type=F32)
    acc = acc + jnp.dot(pn.astype(BF16), vn_ref[...], preferred_element_type=F32)
    o_ref[...] = (acc / l).astype(o_ref.dtype)


def _attn_sample(qn, kpast, vpast, layer, kb, vb, w, n_valid):
    b, tp, _ = qn.shape
    lp = kpast.shape[2]
    nh = w // C_HD
    t = np.arange(tp)
    live = t < n_valid
    bp = jnp.asarray(np.where(live[:, None], _pattern_bias(lp + t[:, None] - np.arange(lp)[None, :]), 0.0),
                     F32)
    bn = jnp.asarray(np.where(live[:, None] & live[None, :], _pattern_bias(t[:, None] - t[None, :]),
                              MASK_BIAS), F32)
    new = pl.BlockSpec((None, tp, C_HD), lambda bb, h: (bb, 0, h))
    past = pl.BlockSpec((None, None, lp, C_HD), lambda bb, h: (layer, bb, 0, h))
    return pl.pallas_call(
        _attn_s_kernel,
        out_shape=jax.ShapeDtypeStruct((b, tp, w), BF16),
        grid=(b, nh),
        in_specs=[new, past, past, new, new,
                  pl.BlockSpec((tp, lp), lambda bb, h: (0, 0)),
                  pl.BlockSpec((tp, tp), lambda bb, h: (0, 0))],
        out_specs=new,
        compiler_params=_cp("parallel", "parallel"),
        name="attn_sample",
    )(qn, kpast, vpast, kb, vb, bp, bn)


def _seg_sum(x, e_ref):
    nb = x.shape[1] // LANE
    e = e_ref[...]
    return jnp.concatenate(
        [jnp.dot(x[:, i * LANE:(i + 1) * LANE], e, precision=HI, preferred_element_type=F32) for i in range(nb)],
        axis=1)


def _rwkv_prep_kernel(r_ref, k_ref, v_ref, x_ref, sr_ref, sk_ref, sv_ref, sx_ref,
                      mr_ref, mk_ref, mv_ref, mx_ref, w0_ref, w2_ref, a0_ref, a2_ref, g2_ref,
                      kk_ref, ka_ref, e_ref,
                      ro_ref, wo_ref, ko_ref, vo_ref, zo_ref, bo_ref, go_ref,
                      pr_ref, pk_ref, pv_ref, px_ref, *, tb, n_valid, masked):
    it = pl.program_id(1)

    @pl.when(it == 0)
    def _():
        pr_ref[...] = sr_ref[...]
        pk_ref[...] = sk_ref[...]
        pv_ref[...] = sv_ref[...]
        px_ref[...] = sx_ref[...]

    first = lax.broadcasted_iota(jnp.int32, (tb, 1), 0) == 0

    def shifted(x_ref_, prev_ref, mu_ref):
        x = x_ref_[...]
        prev = jnp.where(first, prev_ref[...], pltpu.roll(x, 1, axis=0))
        prev_ref[...] = x[tb - 1:tb, :]
        return x + mu_ref[...] * (prev - x)

    r = shifted(r_ref, pr_ref, mr_ref)
    k = shifted(k_ref, pk_ref, mk_ref)
    v = shifted(v_ref, pv_ref, mv_ref)
    x = shifted(x_ref, px_ref, mx_ref)
    lo = x[:, 0:LANE]
    hi = x[:, LANE:]
    mw = jnp.dot(jnp.tanh(lo).astype(BF16), w2_ref[...], preferred_element_type=F32)
    ma = jnp.dot(lo.astype(BF16), a2_ref[...], preferred_element_type=F32)
    g = jnp.dot(_sigmoid(hi).astype(BF16), g2_ref[...], preferred_element_type=F32)
    log_w = -_softplus(-(w0_ref[...] + mw)) - 0.5
    decay = jnp.exp(-jnp.exp(log_w))
    a = _sigmoid(a0_ref[...] + ma)
    kk = k * kk_ref[...]
    kk = kk / jnp.maximum(jnp.sqrt(_seg_sum(kk * kk, e_ref)), 1e-12)
    k2 = k * (1.0 + (a - 1.0) * ka_ref[...])
    bvec = kk * a
    if masked:
        live = (it * tb + lax.broadcasted_iota(jnp.int32, (tb, 1), 0)) < n_valid
        decay = jnp.where(live, decay, 1.0)
        k2 = jnp.where(live, k2, 0.0)
        bvec = jnp.where(live, bvec, 0.0)
    for ref, val in ((ro_ref, r), (wo_ref, decay), (ko_ref, k2), (zo_ref, -kk), (bo_ref, bvec)):
        for c in range(ref.shape[0]):
            ref[c] = val
    vo_ref[...] = v
    go_ref[...] = g


def _rwkv_prep(proj3, shift, pw, w, xw, n_valid, ncopy):
    b, t, _ = proj3.shape
    tb = _pick(t, TIME_BLOCK_CAP, SUBLANE)
    xcol = (12 * w) // xw
    big = lambda c: pl.BlockSpec((None, tb, w), lambda bb, i: (bb, i, c))
    st = lambda n: pl.BlockSpec((None, 1, n), lambda bb, i: (bb, 0, 0))
    full = lambda shape: pl.BlockSpec(shape, lambda bb, i: (0,) * len(shape))
    one = pl.BlockSpec((None, tb, w), lambda bb, i: (bb, i, 0))
    cop = pl.BlockSpec((ncopy, None, tb, w), lambda bb, i: (0, bb, i, 0))
    one_s = jax.ShapeDtypeStruct((b, t, w), F32)
    cop_s = jax.ShapeDtypeStruct((ncopy, b, t, w), F32)
    sr, sk, sv, sx = shift
    r4, w4, k4, v3, z4, b4, g3 = pl.pallas_call(
        functools.partial(_rwkv_prep_kernel, tb=tb, n_valid=n_valid, masked=n_valid < t),
        out_shape=(cop_s, cop_s, cop_s, one_s, cop_s, cop_s, one_s),
        grid=(b, t // tb),
        in_specs=[big(9), big(10), big(11), pl.BlockSpec((None, tb, xw), lambda bb, i: (bb, i, xcol)),
                  st(w), st(w), st(w), st(xw),
                  full((1, w)), full((1, w)), full((1, w)), full((1, xw)),
                  full((1, w)), full((LANE, w)), full((1, w)), full((LANE, w)), full((xw - LANE, w)),
                  full((1, w)), full((1, w)), full((LANE, LANE))],
        out_specs=(cop, cop, cop, one, cop, cop, one),
        scratch_shapes=[pltpu.VMEM((1, w), F32)] * 3 + [pltpu.VMEM((1, xw), F32)],
        compiler_params=_cp("parallel", "arbitrary"),
        name="rwkv_prep",
    )(proj3, proj3, proj3, proj3, sr, sk, sv, sx, *pw)
    return r4, w4, k4, z4, b4, v3, g3


def _rwkv_scan_kernel(r_ref, w_ref, k_ref, z_ref, b_ref, v_ref, s0_ref, o_ref, so_ref, s_ref, *, tc):
    nk, nv = s_ref.shape[0], s_ref.shape[1]
    vg = min(nv, SCAN_V_GROUP)
    groups = [slice(g0, g0 + vg) for g0 in range(0, nv, vg)]
    kc = min(nk, SCAN_K_CHUNK)
    zero = jnp.zeros((vg, LANE), F32)

    @pl.when(pl.program_id(0) == 0)
    def _():
        s_ref[...] = s0_ref[...]

    def tree(parts):
        parts = list(parts)
        while len(parts) > 1:
            parts = [sum(parts[i:i + 2][1:], parts[i]) for i in range(0, len(parts), 2)]
        return parts[0]

    def step(t, carry):
        def row(ref, kx):
            return ref[t, pl.ds(kx, 1), :]

        def pass1(g, c, acc):
            acc = list(acc)
            for j in range(kc):
                kx = c * kc + j
                acc[j % 4] = acc[j % 4] + s_ref[kx, g, :] * row(z_ref, kx)
            return tuple(acc)

        def pass2(g, sz, vv, c, acc):
            acc = list(acc)
            for j in range(kc):
                kx = c * kc + j
                sn = s_ref[kx, g, :] * row(w_ref, kx) + sz * row(b_ref, kx) + vv * row(k_ref, kx)
                s_ref[kx, g, :] = sn
                acc[j % 4] = acc[j % 4] + sn * row(r_ref, kx)
            return tuple(acc)

        for g in groups:
            sz = tree(lax.fori_loop(0, nk // kc, functools.partial(pass1, g), (zero,) * 4))
            acc = lax.fori_loop(0, nk // kc, functools.partial(pass2, g, sz, v_ref[t, g, :]), (zero,) * 4)
            o_ref[t, g, :] = tree(acc)
        return carry

    lax.fori_loop(0, tc, step, 0)
    so_ref[...] = s_ref[...]


def _rwkv_scan(rl, wl, kl, zl, bl, vl, s0l):
    t, nk, _ = rl.shape
    nv = vl.shape[1]
    tc = _pick(t, SCAN_STEPS_CAP, 1)
    tok = pl.BlockSpec((tc, nk, LANE), lambda i: (i, 0, 0))
    vsp = pl.BlockSpec((tc, nv, LANE), lambda i: (i, 0, 0))
    ssp = pl.BlockSpec((nk, nv, LANE), lambda i: (0, 0, 0))
    return pl.pallas_call(
        functools.partial(_rwkv_scan_kernel, tc=tc),
        out_shape=(jax.ShapeDtypeStruct((t, nv, LANE), F32), jax.ShapeDtypeStruct((nk, nv, LANE), F32)),
        grid=(t // tc,),
        in_specs=[tok, tok, tok, tok, tok, vsp, ssp],
        out_specs=(vsp, ssp),
        scratch_shapes=[pltpu.VMEM((nk, nv, LANE), F32)],
        compiler_params=_cp("arbitrary"),
        name="rwkv_scan",
    )(rl, wl, kl, zl, bl, vl, s0l)


def _rwkv_post_kernel(o_ref, r_ref, k_ref, v_ref, g_ref, lg_ref, lb_ref, rk_ref, e_ref, y_ref):
    o = o_ref[...]
    inv_n = 1.0 / D_HD
    mu = _seg_sum(o, e_ref) * inv_n
    d = o - mu
    var = _seg_sum(d * d, e_ref) * inv_n
    on = d * lax.rsqrt(var + RWKV_GN_EPS) * lg_ref[...] + lb_ref[...]
    bonus = _seg_sum(r_ref[...] * k_ref[...] * rk_ref[...], e_ref) * v_ref[...]
    y_ref[...] = ((on + bonus) * g_ref[...]).astype(y_ref.dtype)


def _rwkv_post(o3, r3, k3, v3, g3, lg, lb, rk, e):
    b, t, w = o3.shape
    tb = _pick(t, TIME_BLOCK_CAP, SUBLANE)
    big = pl.BlockSpec((None, tb, w), lambda bb, i: (bb, i, 0))
    rowsp = pl.BlockSpec((1, w), lambda bb, i: (0, 0))
    return pl.pallas_call(
        _rwkv_post_kernel,
        out_shape=jax.ShapeDtypeStruct((b, t, w), BF16),
        grid=(b, t // tb),
        in_specs=[big] * 5 + [rowsp] * 3 + [pl.BlockSpec((LANE, LANE), lambda bb, i: (0, 0))],
        out_specs=big,
        compiler_params=_cp("parallel", "parallel"),
        name="rwkv_post",
    )(o3, r3, k3, v3, g3, lg, lb, rk, e)


def _rwkv(proj3, shift, s0, pw, w, xw, n_valid):
    b, t, _ = proj3.shape
    nh = w // D_HD
    chains = b * nh
    assert LANE % chains == 0 and D_HD % (LANE // chains) == 0
    vh = LANE // chains
    nv = D_HD // vh
    prep_w, (lg, lb, rk, e) = pw
    r4, w4, k4, z4, b4, v3, g3 = _rwkv_prep(proj3, shift, prep_w, w, xw, n_valid, vh)
    r3, k3 = r4[0], k4[0]

    def klay(x4):
        return x4.reshape(vh, b, t, nh, D_HD).transpose(2, 4, 0, 1, 3).reshape(t, D_HD, LANE)

    def vlay(x3):
        return x3.reshape(b, t, nh, vh, nv).transpose(1, 4, 3, 0, 2).reshape(t, nv, LANE)

    s0l = s0.reshape(b, nh, vh, nv, D_HD).transpose(4, 3, 2, 0, 1).reshape(D_HD, nv, LANE)
    ol, sl = _rwkv_scan(klay(r4), klay(w4), klay(k4), klay(z4), klay(b4), vlay(v3), s0l)
    o3 = ol.reshape(t, nv, vh, b, nh).transpose(3, 0, 4, 2, 1).reshape(b, t, w)
    s_new = sl.reshape(D_HD, nv, vh, b, nh).transpose(3, 4, 2, 1, 0).reshape(b, nh, D_HD, D_HD)
    y = _rwkv_post(o3, r3, k3, v3, g3, lg, lb, rk, e)
    return y, s_new


def _layer(x3, n_valid, tables, state, lbv, p, dims):
    d, w, xw, n_in = dims
    b, t, _ = x3.shape
    k_past, v_past, s_a, h_b, buf_b, s_d, shift_d, buf_f = state
    m = b * t
    x2 = x3.reshape(m, d)
    h = _rmsnorm_bf16(x2, p['ln_mix_g'])
    big, layer = p['big'], p['layer']
    proj = _matmul(h, big['w_in'], layer, tm_cap=MM_TM_CAP, tn_cap=1152, tk_cap=d, w_resident=True)
    proj3 = proj.reshape(b, t, -1)
    y_a, s_a_t = _hgrn(proj3, lbv, p['hgrn_norm_g'].astype(F32), jnp.swapaxes(s_a, -1, -2), w, n_valid)
    s_a_new = jnp.swapaxes(s_a_t, -1, -2)
    y_b, h_b_new, buf_b_new = _lru(proj3, buf_b, h_b, p, w, n_valid)
    qn, kn, v_new, kb, vb = _qk_prep(proj3, p['attn_q_norm_g'], p['attn_k_norm_g'], tables, w)
    if k_past is None:
        y_c = _attn_prompt(qn, kb, vb, w)
    else:
        (k_all, layer), (v_all, _) = k_past, v_past
        y_c = _attn_sample(qn, k_all, v_all, layer, kb, vb, w, n_valid)
    sh = shift_d.reshape(b, 1, -1)
    shift = (sh[:, :, 0:w], sh[:, :, w:2 * w], sh[:, :, 2 * w:3 * w],
             jnp.pad(sh[:, :, 3 * w:], ((0, 0), (0, 0), (0, xw - (sh.shape[2] - 3 * w)))))
    y_d, s_d_new = _rwkv(proj3, shift, s_d, p['rwkv'], w, xw, n_valid)
    shift_new = proj3[:, n_valid - 1, 9 * w:n_in]
    y_mix = jnp.concatenate([y_a, y_b, y_c, y_d], axis=-1).reshape(m, 4 * w)
    x2 = _matmul(y_mix, big['w_out'], layer, x2, tm_cap=MM_TM_CAP, tn_cap=512, tk_cap=4 * w, w_resident=True)
    h = _rmsnorm_bf16(x2, p['ln_ffn_g'])
    z, buf_f_new = _ffn_gate_up(h.reshape(b, t, d), big['ffn_w_gate'], big['ffn_w_up'], layer, p['ffn_conv_w'],
                                buf_f, n_valid)
    f = z.shape[-1]
    x2 = _matmul(z.reshape(m, f), big['ffn_w_down'], layer, x2, tm_cap=MM_TM_CAP, tn_cap=512, tk_cap=f // 2,
                 w_resident=False)
    keep = slice(max(n_valid - C_WIN, 0), n_valid) if k_past is None else slice(0, n_valid)
    new_state = (kn[:, keep], v_new[:, keep], s_a_new, h_b_new.reshape(b, w), buf_b_new,
                 s_d_new, shift_new, buf_f_new)
    return x2.reshape(b, t, d), new_state


def kernel(x_prompt, x_sample, cache_attn_k, cache_attn_v, state_hgrn, state_lru_h, state_lru_conv, state_rwkv, state_rwkv_shift, state_ffn_conv, ln_mix_g, w_in, hgrn_lb_logits, hgrn_norm_g, lru_conv_w, lru_conv_b, lru_wa, lru_ba, lru_wx, lru_bx, lru_lambda, attn_q_norm_g, attn_k_norm_g, rwkv_mu, rwkv_w0, rwkv_w2, rwkv_a0, rwkv_a2, rwkv_g2, rwkv_k_k, rwkv_k_a, rwkv_r_k, rwkv_ln_g, rwkv_ln_b, w_out, ln_ffn_g, ffn_w_gate, ffn_w_up, ffn_conv_w, ffn_w_down):
    bp, tp, d = x_prompt.shape
    bs, ts, _ = x_sample.shape
    depth = w_in.shape[0]
    n_in = w_in.shape[2]
    w = d // 4
    n_pad = -(-n_in // LANE) * LANE
    rank_w, rank_a, rank_g = rwkv_w2.shape[1], rwkv_a2.shape[1], rwkv_g2.shape[1]
    assert rank_w + rank_a == LANE
    xw = n_pad - 12 * w
    dims = (d, w, xw, n_in)
    row = lambda v: v.reshape(1, -1).astype(F32)

    lb_soft = jax.nn.softmax(hgrn_lb_logits.astype(F32), axis=0)
    lb_all = jnp.cumsum(lb_soft, axis=0) - lb_soft[:1]

    big = {
        'w_in': jnp.pad(w_in, ((0, 0), (0, 0), (0, n_pad - n_in))).astype(BF16),
        'w_out': w_out.astype(BF16),
        'ffn_w_gate': ffn_w_gate.astype(BF16), 'ffn_w_up': ffn_w_up.astype(BF16),
        'ffn_w_down': ffn_w_down.astype(BF16),
    }
    zeros = lambda r: jnp.zeros((r, w), F32)
    head_id = np.arange(LANE) // D_HD
    e_blk = jnp.asarray((head_id[:, None] == head_id[None, :]).astype(np.float32))

    layers = []
    for l in range(depth):
        mu = rwkv_mu[l].astype(F32)
        prep_w = (
            row(mu[0:w]), row(mu[w:2 * w]), row(mu[2 * w:3 * w]),
            row(jnp.pad(mu[3 * w:], (0, xw - (mu.shape[0] - 3 * w)))),
            row(rwkv_w0[l]),
            jnp.concatenate([rwkv_w2[l].astype(F32), zeros(rank_a)], axis=0).astype(BF16),
            row(rwkv_a0[l]),
            jnp.concatenate([zeros(rank_w), rwkv_a2[l].astype(F32)], axis=0).astype(BF16),
            jnp.concatenate([rwkv_g2[l].astype(F32), zeros(xw - LANE - rank_g)], axis=0).astype(BF16),
            row(rwkv_k_k[l]), row(rwkv_k_a[l]), e_blk,
        )
        post_w = (row(rwkv_ln_g[l]), row(rwkv_ln_b[l]), row(rwkv_r_k[l]), e_blk)
        layers.append({
            'ln_mix_g': ln_mix_g[l], 'big': big, 'layer': l, 'hgrn_norm_g': hgrn_norm_g[l],
            'lru_conv_w': lru_conv_w[l], 'lru_conv_b': lru_conv_b[l], 'lru_wa': lru_wa[l],
            'lru_ba': lru_ba[l], 'lru_wx': lru_wx[l], 'lru_bx': lru_bx[l], 'lru_lambda': lru_lambda[l],
            'attn_q_norm_g': attn_q_norm_g[l], 'attn_k_norm_g': attn_k_norm_g[l],
            'rwkv': (prep_w, post_w), 'ln_ffn_g': ln_ffn_g[l], 'ffn_conv_w': ffn_conv_w[l].astype(F32),
        })

    f = ffn_w_gate.shape[2]
    nh_a, nh_d = w // A_HD, w // D_HD
    zero_state = (None, None, jnp.zeros((bp, nh_a, A_HD, A_HD), F32), jnp.zeros((bp, w), F32),
                  jnp.zeros((bp, 3, w), F32), jnp.zeros((bp, nh_d, D_HD, D_HD), F32),
                  jnp.zeros((bp, n_in - 9 * w), F32), jnp.zeros((bp, 2, f), F32))
    tab_p = _rope_tables(jnp.arange(tp, dtype=jnp.int32))
    tab_s = _rope_tables(PAST_LEN + jnp.arange(PAD_T, dtype=jnp.int32))
    lp = cache_attn_k.shape[2]
    cache_k = cache_attn_k.reshape(depth, bs, lp, w)
    cache_v = cache_attn_v.reshape(depth, bs, lp, w)
    xp = x_prompt.astype(F32)
    xs = jnp.pad(x_sample.astype(F32), ((0, 0), (0, PAD_T - ts), (0, 0)))
    p_states, s_states = [], []
    for l in range(depth):
        xp, st_p = _layer(xp, tp, tab_p, zero_state, lb_all[l], layers[l], dims)
        p_states.append(st_p)
        s_in = ((cache_k, l), (cache_v, l), state_hgrn[l].astype(F32),
                state_lru_h[l].astype(F32), state_lru_conv[l].astype(F32), state_rwkv[l].astype(F32),
                state_rwkv_shift[l].astype(F32), state_ffn_conv[l].astype(F32))
        xs, st_s = _layer(xs, ts, tab_s, s_in, lb_all[l], layers[l], dims)
        s_states.append(st_s)

    dp, ds = x_prompt.dtype, x_sample.dtype
    nh_c = w // C_HD

    def stack(states, i, dt, shape=None):
        out = jnp.stack([st[i] for st in states]).astype(dt)
        return out if shape is None else out.reshape((depth,) + shape)

    outs = [xp.astype(dp), xs[:, :ts].astype(ds)]
    for states, bb, tt, dt in ((p_states, bp, min(C_WIN, tp), dp), (s_states, bs, ts, ds)):
        outs += [stack(states, 0, dt, (bb, tt, nh_c, C_HD)), stack(states, 1, dt, (bb, tt, nh_c, C_HD)),
                 stack(states, 2, dt), stack(states, 3, dt), stack(states, 4, dt), stack(states, 5, dt),
                 stack(states, 6, dt), stack(states, 7, dt)]
    return tuple(outs)
```
